```python
import jax, jax.numpy as jnp
from jax import lax
import numpy as np

D_MODEL = 1024
BATCH = 4
SEQ = 4096
DEPTH = 1
DEC_BATCH = 128
DEC_SEQ = 1
PAST_LEN = 8192
PAGE_SIZE = 128

HEAD_DIM = 64
N_FOX_HEADS = 8
FOX_WIDTH = N_FOX_HEADS * HEAD_DIM
CONV_CH = D_MODEL // 4
CONV_WIDTH = 31
N_MEM = 256
N_MEM_HEADS = 4
MEM_WIDTH = N_MEM_HEADS * HEAD_DIM
N_BRANCH = 3
D_FF = 3 * D_MODEL
FFN_CONV_WIDTH = 3
Q_BLOCK = 128
EPS = 1e-6
ATTN_SCALE = HEAD_DIM ** -0.5
NEG_INF = -1e30
IN_COLS = 3 * FOX_WIDTH + N_FOX_HEADS + 2 * CONV_CH + MEM_WIDTH + N_BRANCH * D_MODEL

kernel_name = "fox_conformer_memory_gated_hybrid_step"


def rmsnorm(x, g):
    xf = x.astype(jnp.float32)
    y = xf * lax.rsqrt(jnp.mean(xf * xf, axis=-1, keepdims=True) + EPS)
    return (y * g.astype(jnp.float32)).astype(x.dtype)


def layernorm(x, g, b):
    xf = x.astype(jnp.float32)
    mu = jnp.mean(xf, axis=-1, keepdims=True)
    var = jnp.mean(jnp.square(xf - mu), axis=-1, keepdims=True)
    y = (xf - mu) * lax.rsqrt(var + EPS) * g.astype(jnp.float32) + b.astype(jnp.float32)
    return y.astype(x.dtype)


def causal_dwconv(buf, w):
    ch = w.shape[1]
    return lax.conv_general_dilated(buf, w[:, None, :].astype(buf.dtype), window_strides=(1,), padding='VALID',
                                    dimension_numbers=('NWC', 'WIO', 'NWC'), feature_group_count=ch)


def mixer_front(x, g_mix, w_in, b_f):
    b, t, _ = x.shape
    h = rmsnorm(x, g_mix)
    p = jnp.einsum('btd,dc->btc', h, w_in)
    sizes = [FOX_WIDTH, FOX_WIDTH, FOX_WIDTH, N_FOX_HEADS, 2 * CONV_CH, MEM_WIDTH, N_BRANCH * D_MODEL]
    offsets = [int(o) for o in np.cumsum(sizes)[:-1]]
    q, k, v, f_logit, glu_in, q_mem, gate_logit = jnp.split(p, offsets, axis=-1)
    q = q.reshape(b, t, N_FOX_HEADS, HEAD_DIM)
    k = k.reshape(b, t, N_FOX_HEADS, HEAD_DIM)
    v = v.reshape(b, t, N_FOX_HEADS, HEAD_DIM)
    logf = jax.nn.log_sigmoid(f_logit.astype(jnp.float32) + b_f.astype(jnp.float32))
    a, gl = jnp.split(glu_in, 2, axis=-1)
    u = a * jax.nn.sigmoid(gl)
    q_mem = q_mem.reshape(b, t, N_MEM_HEADS, HEAD_DIM)
    return q, k, v, logf, u, q_mem, gate_logit


def memory_kv(mem, g_mem, w_mem_kv):
    b, n, _ = mem.shape
    kv = jnp.einsum('bnd,dc->bnc', rmsnorm(mem, g_mem), w_mem_kv)
    mk, mv = jnp.split(kv, 2, axis=-1)
    return mk.reshape(b, n, N_MEM_HEADS, HEAD_DIM), mv.reshape(b, n, N_MEM_HEADS, HEAD_DIM)


def fox_attend(q, cq, qpos, k, v, ck, kpos):
    s = jnp.einsum('bqhd,bkhd->bhqk', q, k).astype(jnp.float32) * ATTN_SCALE
    bias = jnp.swapaxes(cq, 1, 2)[..., :, None] - jnp.swapaxes(ck, 1, 2)[..., None, :]
    s = jnp.where(kpos[None, :] <= qpos[:, None], s + bias, NEG_INF)
    p = jax.nn.softmax(s, axis=-1).astype(v.dtype)
    return jnp.einsum('bhqk,bkhd->bqhd', p, v)


def fox_prompt(q, k, v, logf):
    b, t, h, d = q.shape
    nb = t // Q_BLOCK
    c = jnp.cumsum(logf, axis=1)
    kpos = jnp.arange(t)
    qb = jnp.swapaxes(q.reshape(b, nb, Q_BLOCK, h, d), 0, 1)
    cb = jnp.swapaxes(c.reshape(b, nb, Q_BLOCK, h), 0, 1)
    pb = kpos.reshape(nb, Q_BLOCK)

    def one_block(args):
        qi, ci, pi = args
        return fox_attend(qi, ci, pi, k, v, c, kpos)

    o = lax.map(one_block, (qb, cb, pb))
    return jnp.swapaxes(o, 0, 1).reshape(b, t, h * d)


def fox_sample(q, k, v, logf, past_k, past_v, past_logf):
    b, t, h, d = q.shape
    past = past_k.shape[1]
    c = jnp.cumsum(jnp.concatenate([past_logf.astype(jnp.float32), logf], axis=1), axis=1)
    cq = jnp.swapaxes(c[:, past:], 1, 2)[..., :, None]
    ck = jnp.swapaxes(c, 1, 2)[..., None, :]
    s_past = jnp.einsum('bqhd,bkhd->bhqk', q, past_k).astype(jnp.float32) * ATTN_SCALE
    s_new = jnp.einsum('bqhd,bkhd->bhqk', q, k).astype(jnp.float32) * ATTN_SCALE
    s = jnp.concatenate([s_past, s_new], axis=-1) + (cq - ck)
    qpos = jnp.arange(t)
    mask = jnp.concatenate([jnp.ones((t, past), dtype=bool), qpos[None, :] <= qpos[:, None]], axis=-1)
    s = jnp.where(mask, s, NEG_INF)
    p = jax.nn.softmax(s, axis=-1).astype(v.dtype)
    o = (jnp.einsum('bhqk,bkhd->bqhd', p[..., :past], past_v)
         + jnp.einsum('bhqk,bkhd->bqhd', p[..., past:], v))
    return o.reshape(b, t, h * d)


def memory_attend(q, mk, mv):
    b, t = q.shape[0], q.shape[1]
    s = jnp.einsum('bqhd,bkhd->bhqk', q, mk).astype(jnp.float32) * ATTN_SCALE
    p = jax.nn.softmax(s, axis=-1).astype(mv.dtype)
    return jnp.einsum('bhqk,bkhd->bqhd', p, mv).reshape(b, t, MEM_WIDTH)


def layer_tail(x, y_fox, u, q_mem, gate_logit, conv_state, ffn_state, mem_k, mem_v,
               w_conv_dw, g_conv_norm, b_conv_norm, w_a, w_b, w_c, w_o,
               g_ffn, w_ffn_in, w_ffn_conv, w_down):
    conv_buf = jnp.concatenate([conv_state.astype(u.dtype), u], axis=1)
    yb = jax.nn.silu(layernorm(causal_dwconv(conv_buf, w_conv_dw), g_conv_norm, b_conv_norm))
    new_conv_state = conv_buf[:, -(CONV_WIDTH - 1):]
    yc = memory_attend(q_mem, mem_k, mem_v)
    g_a, g_b, g_c = jnp.split(jax.nn.sigmoid(gate_logit), N_BRANCH, axis=-1)
    merged = g_a * (y_fox @ w_a) + g_b * (yb @ w_b) + g_c * (yc @ w_c)
    x = x + merged @ w_o
    gt, up = jnp.split(rmsnorm(x, g_ffn) @ w_ffn_in, 2, axis=-1)
    ffn_buf = jnp.concatenate([ffn_state.astype(gt.dtype), gt], axis=1)
    x = x + (jax.nn.silu(causal_dwconv(ffn_buf, w_ffn_conv)) * up) @ w_down
    new_ffn_state = ffn_buf[:, -(FFN_CONV_WIDTH - 1):]
    return x, new_conv_state, new_ffn_state


def setup_inputs(seed: int = 0) -> dict:
    key = jax.random.key(seed)
    ks = jax.random.split(key, 40)
    n_pages = PAST_LEN // PAGE_SIZE
    n_used = DEC_BATCH * n_pages
    n_pool = n_used + (n_used + 3) // 4

    def nrm(k, shape, scale=1.0):
        return scale * jax.random.normal(k, shape, jnp.float32)

    def gain(k, shape):
        return 1.0 + 0.05 * jax.random.normal(k, shape, jnp.float32)

    page_table = jax.random.permutation(ks[0], n_pool)[:n_used].reshape(DEC_BATCH, n_pages).astype(jnp.int32)
    return {
        'x_prompt': nrm(ks[1], (BATCH, SEQ, D_MODEL)),
        'x_sample': nrm(ks[2], (DEC_BATCH, DEC_SEQ, D_MODEL)),
        'cache_fox_k': nrm(ks[3], (DEPTH, n_pool, PAGE_SIZE, N_FOX_HEADS, HEAD_DIM)),
        'cache_fox_v': nrm(ks[4], (DEPTH, n_pool, PAGE_SIZE, N_FOX_HEADS, HEAD_DIM)),
        'cache_fox_logf': jax.nn.log_sigmoid(jax.random.uniform(ks[5], (DEPTH, n_pool, PAGE_SIZE, N_FOX_HEADS), jnp.float32, 1.0, 6.0)
                                             + nrm(ks[6], (DEPTH, n_pool, PAGE_SIZE, N_FOX_HEADS))),
        'state_conv': nrm(ks[7], (DEPTH, DEC_BATCH, CONV_WIDTH - 1, CONV_CH), 0.5),
        'state_ffn_conv': nrm(ks[8], (DEPTH, DEC_BATCH, FFN_CONV_WIDTH - 1, D_FF)),
        'cache_mem_k': nrm(ks[9], (DEPTH, DEC_BATCH, N_MEM, N_MEM_HEADS, HEAD_DIM)),
        'cache_mem_v': nrm(ks[10], (DEPTH, DEC_BATCH, N_MEM, N_MEM_HEADS, HEAD_DIM)),
        'page_table': page_table,
        'mem_prompt': nrm(ks[11], (BATCH, N_MEM, D_MODEL)),
        'g_mix': gain(ks[12], (DEPTH, D_MODEL)),
        'w_in': nrm(ks[13], (DEPTH, D_MODEL, IN_COLS), D_MODEL ** -0.5),
        'b_f': jax.random.uniform(ks[14], (DEPTH, N_FOX_HEADS), jnp.float32, 1.0, 6.0),
        'w_conv_dw': nrm(ks[15], (DEPTH, CONV_WIDTH, CONV_CH), CONV_WIDTH ** -0.5),
        'g_conv_norm': gain(ks[16], (DEPTH, CONV_CH)),
        'b_conv_norm': nrm(ks[17], (DEPTH, CONV_CH), 0.02),
        'g_mem': gain(ks[18], (DEPTH, D_MODEL)),
        'w_mem_kv': nrm(ks[19], (DEPTH, D_MODEL, 2 * MEM_WIDTH), D_MODEL ** -0.5),
        'w_a': nrm(ks[20], (DEPTH, FOX_WIDTH, D_MODEL), FOX_WIDTH ** -0.5),
        'w_b': nrm(ks[21], (DEPTH, CONV_CH, D_MODEL), CONV_CH ** -0.5),
        'w_c': nrm(ks[22], (DEPTH, MEM_WIDTH, D_MODEL), MEM_WIDTH ** -0.5),
        'w_o': nrm(ks[23], (DEPTH, D_MODEL, D_MODEL), D_MODEL ** -0.5),
        'g_ffn': gain(ks[24], (DEPTH, D_MODEL)),
        'w_ffn_in': nrm(ks[25], (DEPTH, D_MODEL, 2 * D_FF), D_MODEL ** -0.5),
        'w_ffn_conv': nrm(ks[26], (DEPTH, FFN_CONV_WIDTH, D_FF), FFN_CONV_WIDTH ** -0.5),
        'w_down': nrm(ks[27], (DEPTH, D_FF, D_MODEL), D_FF ** -0.5),
        'g_final': gain(ks[28], (D_MODEL,)),
    }


def reference(x_prompt, x_sample, cache_fox_k, cache_fox_v, cache_fox_logf, state_conv, state_ffn_conv,
              cache_mem_k, cache_mem_v, page_table, mem_prompt,
              g_mix, w_in, b_f, w_conv_dw, g_conv_norm, b_conv_norm, g_mem, w_mem_kv,
              w_a, w_b, w_c, w_o, g_ffn, w_ffn_in, w_ffn_conv, w_down, g_final):
    xp, xs = x_prompt, x_sample
    bp, bs = xp.shape[0], xs.shape[0]
    kp_l, vp_l, lfp_l, csp_l, fsp_l, mkp_l, mvp_l = [], [], [], [], [], [], []
    ks_l, vs_l, lfs_l, css_l, fss_l = [], [], [], [], []
    for l in range(DEPTH):
        q, k, v, logf, u, qm, gate_logit = mixer_front(xp, g_mix[l], w_in[l], b_f[l])
        mk, mv = memory_kv(mem_prompt, g_mem[l], w_mem_kv[l])
        y_fox = fox_prompt(q, k, v, logf)
        conv0 = jnp.zeros((bp, CONV_WIDTH - 1, CONV_CH), xp.dtype)
        ffn0 = jnp.zeros((bp, FFN_CONV_WIDTH - 1, D_FF), xp.dtype)
        xp, cs, fs = layer_tail(xp, y_fox, u, qm, gate_logit, conv0, ffn0, mk, mv,
                                w_conv_dw[l], g_conv_norm[l], b_conv_norm[l], w_a[l], w_b[l], w_c[l], w_o[l],
                                g_ffn[l], w_ffn_in[l], w_ffn_conv[l], w_down[l])
        kp_l.append(k); vp_l.append(v); lfp_l.append(logf)
        csp_l.append(cs); fsp_l.append(fs); mkp_l.append(mk); mvp_l.append(mv)
        q, k, v, logf, u, qm, gate_logit = mixer_front(xs, g_mix[l], w_in[l], b_f[l])
        past_k = cache_fox_k[l, page_table].reshape(bs, -1, N_FOX_HEADS, HEAD_DIM)
        past_v = cache_fox_v[l, page_table].reshape(bs, -1, N_FOX_HEADS, HEAD_DIM)
        past_lf = cache_fox_logf[l, page_table].reshape(bs, -1, N_FOX_HEADS)
        y_fox = fox_sample(q, k, v, logf, past_k, past_v, past_lf)
        xs, cs, fs = layer_tail(xs, y_fox, u, qm, gate_logit, state_conv[l], state_ffn_conv[l],
                                cache_mem_k[l], cache_mem_v[l],
                                w_conv_dw[l], g_conv_norm[l], b_conv_norm[l], w_a[l], w_b[l], w_c[l], w_o[l],
                                g_ffn[l], w_ffn_in[l], w_ffn_conv[l], w_down[l])
        ks_l.append(k); vs_l.append(v); lfs_l.append(logf); css_l.append(cs); fss_l.append(fs)
    y_prompt = rmsnorm(xp, g_final)
    y_sample = rmsnorm(xs, g_final)
    fox_k_p, fox_v_p, fox_logf_p = jnp.stack(kp_l), jnp.stack(vp_l), jnp.stack(lfp_l)
    conv_state_p, ffn_state_p = jnp.stack(csp_l), jnp.stack(fsp_l)
    mem_k_p, mem_v_p = jnp.stack(mkp_l), jnp.stack(mvp_l)
    fox_k_s, fox_v_s, fox_logf_s = jnp.stack(ks_l), jnp.stack(vs_l), jnp.stack(lfs_l)
    conv_state_s, ffn_state_s = jnp.stack(css_l), jnp.stack(fss_l)
    return (y_prompt, y_sample, fox_k_p, fox_v_p, fox_logf_p, conv_state_p, ffn_state_p, mem_k_p, mem_v_p,
            fox_k_s, fox_v_s, fox_logf_s, conv_state_s, ffn_state_s)
```

```python
import functools

import jax
import jax.numpy as jnp
from jax import lax
from jax.experimental import pallas as pl
from jax.experimental.pallas import tpu as pltpu

F32 = jnp.float32
BF16 = jnp.bfloat16

D_MODEL = 1024
HEAD_DIM = 64
N_FOX_HEADS = 8
FOX_WIDTH = N_FOX_HEADS * HEAD_DIM
CONV_CH = 256
CONV_WIDTH = 31
N_MEM = 256
N_MEM_HEADS = 4
MEM_WIDTH = N_MEM_HEADS * HEAD_DIM
D_FF = 3 * D_MODEL
FFN_CONV_WIDTH = 3
PAGE_SIZE = 128
EPS = 1e-6
ATTN_SCALE = HEAD_DIM ** -0.5
NEG_INF = -1e30

LANES = 128
ROW_TILE = 256
ATTN_BLOCK = 256
CONV_HALO = 32
FFN_HALO = 16
FF_CHUNK = 512
PAGES_PER_STEP = 8
VMEM_LIMIT_BYTES = 56 * 1024 * 1024


def _params(n_axes):
    return pltpu.CompilerParams(dimension_semantics=("arbitrary",) * n_axes,
                                vmem_limit_bytes=VMEM_LIMIT_BYTES)


def _resident(shape):
    nd = len(shape)
    return pl.BlockSpec(shape, lambda *_: (0,) * nd, pipeline_mode=pl.Buffered(1))


def _rms_scale(x):
    return x * lax.rsqrt(jnp.mean(x * x, axis=-1, keepdims=True) + EPS)


def _sigmoid(x):
    return 1.0 / (1.0 + jnp.exp(-x))


def _split3(x):
    hi = x.astype(BF16)
    r1 = x - hi.astype(F32)
    mid = r1.astype(BF16)
    lo = (r1 - mid.astype(F32)).astype(BF16)
    return hi, mid, lo


def _dot3(a, b01):
    hi, mid, lo = _split3(a)
    return (jnp.dot(hi, b01, preferred_element_type=F32)
            + jnp.dot(mid, b01, preferred_element_type=F32)
            + jnp.dot(lo, b01, preferred_element_type=F32))


def _dot3_left(a01, b):
    hi, mid, lo = _split3(b)
    return (jnp.dot(a01, hi, preferred_element_type=F32)
            + jnp.dot(a01, mid, preferred_element_type=F32)
            + jnp.dot(a01, lo, preferred_element_type=F32))


def _head_rows(row, n_rows, width):
    r = lax.broadcasted_iota(jnp.int32, (n_rows, width), 0)
    c = lax.broadcasted_iota(jnp.int32, (n_rows, width), 1)
    keep = (c >= r * HEAD_DIM) & (c < (r + 1) * HEAD_DIM)
    return jnp.where(keep, jnp.broadcast_to(row.astype(F32), (n_rows, width)), 0.0)


def _front_kernel(x_ref, gmix_ref, wqkv_ref, wf_ref, bf_ref, wglu_ref, wqm_ref, wgate_ref,
                  q_ref, k_ref, v_ref, kb_ref, vb_ref, lf_ref, c_ref, u_ref, qm_ref, g_ref,
                  carry_ref, *, tiles_per_seq):
    i = pl.program_id(0)
    tm = x_ref.shape[0]
    h = (_rms_scale(x_ref[...]) * gmix_ref[...]).astype(BF16)

    qkv = jnp.dot(h, wqkv_ref[...], preferred_element_type=F32)
    q_ref[...] = (qkv[:, :FOX_WIDTH] * ATTN_SCALE).astype(BF16)
    k = qkv[:, FOX_WIDTH:2 * FOX_WIDTH]
    v = qkv[:, 2 * FOX_WIDTH:]
    k_ref[...] = k
    v_ref[...] = v
    kb_ref[...] = k.astype(BF16)
    vb_ref[...] = v.astype(BF16)

    fl = jnp.dot(h, wf_ref[...], preferred_element_type=F32) + bf_ref[...]
    lf = jnp.minimum(fl, 0.0) - jnp.log(1.0 + jnp.exp(-jnp.abs(fl)))
    lf_ref[...] = lf

    @pl.when(i % tiles_per_seq == 0)
    def _():
        carry_ref[...] = jnp.zeros_like(carry_ref)

    r = lax.broadcasted_iota(jnp.int32, (tm, tm), 0)
    c = lax.broadcasted_iota(jnp.int32, (tm, tm), 1)
    lower = jnp.where(c <= r, 1.0, 0.0).astype(BF16)
    csum = _dot3_left(lower, lf) + carry_ref[...]
    c_ref[...] = csum
    carry_ref[...] = csum[tm - 1:tm, :]

    glu = jnp.dot(h, wglu_ref[...], preferred_element_type=F32)
    u_ref[...] = glu[:, :CONV_CH] * _sigmoid(glu[:, CONV_CH:])
    qm = jnp.dot(h, wqm_ref[...], preferred_element_type=F32)
    qm_ref[...] = (qm * ATTN_SCALE).astype(BF16)
    g_ref[...] = _sigmoid(jnp.dot(h, wgate_ref[...], preferred_element_type=F32)).astype(BF16)


def _front(x2d, tiles_per_seq, tm, gmix, wqkv, wf, bfp, wglu, wqm, wgate):
    n = x2d.shape[0]
    rows = lambda w: pl.BlockSpec((tm, w), lambda i: (i, 0))
    out_shape = [
        jax.ShapeDtypeStruct((n, FOX_WIDTH), BF16),
        jax.ShapeDtypeStruct((n, FOX_WIDTH), F32),
        jax.ShapeDtypeStruct((n, FOX_WIDTH), F32),
        jax.ShapeDtypeStruct((n, FOX_WIDTH), BF16),
        jax.ShapeDtypeStruct((n, FOX_WIDTH), BF16),
        jax.ShapeDtypeStruct((n, LANES), F32),
        jax.ShapeDtypeStruct((n, LANES), F32),
        jax.ShapeDtypeStruct((n, CONV_CH), F32),
        jax.ShapeDtypeStruct((n, MEM_WIDTH), BF16),
        jax.ShapeDtypeStruct((n, 3 * D_MODEL), BF16),
    ]
    out_specs = [rows(s.shape[1]) for s in out_shape]
    return pl.pallas_call(
        functools.partial(_front_kernel, tiles_per_seq=tiles_per_seq),
        grid=(n // tm,),
        in_specs=[rows(D_MODEL), _resident(gmix.shape), _resident(wqkv.shape), _resident(wf.shape),
                  _resident(bfp.shape), _resident(wglu.shape), _resident(wqm.shape), _resident(wgate.shape)],
        out_specs=out_specs,
        out_shape=out_shape,
        scratch_shapes=[pltpu.VMEM((1, LANES), F32)],
        compiler_params=_params(1),
        name="front",
    )(x2d, gmix, wqkv, wf, bfp, wglu, wqm, wgate)


def _memkv_kernel(m_ref, g_ref, w_ref, mk_ref, mv_ref, mkb_ref, mvb_ref):
    h = (_rms_scale(m_ref[...]) * g_ref[...]).astype(BF16)
    kv = jnp.dot(h, w_ref[...], preferred_element_type=F32)
    mk = kv[:, :MEM_WIDTH]
    mv = kv[:, MEM_WIDTH:]
    mk_ref[...] = mk
    mv_ref[...] = mv
    mkb_ref[...] = mk.astype(BF16)
    mvb_ref[...] = mv.astype(BF16)


def _memkv(mem2d, gmem, wkv):
    n = mem2d.shape[0]
    tm = N_MEM
    rows = lambda w: pl.BlockSpec((tm, w), lambda i: (i, 0))
    shapes = [jax.ShapeDtypeStruct((n, MEM_WIDTH), F32)] * 2 + [jax.ShapeDtypeStruct((n, MEM_WIDTH), BF16)] * 2
    return pl.pallas_call(
        _memkv_kernel,
        grid=(n // tm,),
        in_specs=[rows(D_MODEL), _resident(gmem.shape), _resident(wkv.shape)],
        out_specs=[rows(MEM_WIDTH)] * 4,
        out_shape=shapes,
        compiler_params=_params(1),
        name="memkv",
    )(mem2d, gmem, wkv)


def _fox_prompt_kernel(q_ref, k_ref, v_ref, ct_ref, o_ref):
    i = pl.program_id(2)
    tq = q_ref.shape[0]
    tk = tq
    q = q_ref[...]
    lane = lax.broadcasted_iota(jnp.int32, (tq, LANES), 1)
    row = lax.broadcasted_iota(jnp.int32, (tq, tk), 0)
    col = lax.broadcasted_iota(jnp.int32, (tq, tk), 1)
    outs = []
    for hh in range(2):
        in_head = (lane >= hh * HEAD_DIM) & (lane < (hh + 1) * HEAD_DIM)
        qh = jnp.where(in_head, q, jnp.zeros((), q.dtype))

        def step(j, carry, masked):
            m, l, acc = carry
            start = pl.multiple_of(j * tk, tk)
            kb = k_ref[pl.ds(start, tk), :]
            vb = v_ref[pl.ds(start, tk), :]
            s = lax.dot_general(qh, kb, (((1,), (1,)), ((), ())), preferred_element_type=F32)
            z = s - ct_ref[0, 0, hh:hh + 1, pl.ds(start, tk)]
            if masked:
                z = jnp.where(col <= row, z, NEG_INF)
            m_new = jnp.maximum(m, jnp.max(z, axis=-1, keepdims=True))
            alpha = jnp.exp(m - m_new)
            p = jnp.exp(z - m_new)
            l = alpha * l + jnp.sum(p, axis=-1, keepdims=True)
            acc = alpha * acc + jnp.dot(p.astype(BF16), vb, preferred_element_type=F32)
            return m_new, l, acc

        init = (jnp.full((tq, 1), NEG_INF, F32), jnp.zeros((tq, 1), F32), jnp.zeros((tq, LANES), F32))
        carry = lax.fori_loop(0, i, functools.partial(step, masked=False), init)
        m, l, acc = step(i, carry, True)
        outs.append(acc / l)
    o_ref[...] = jnp.where(lane < HEAD_DIM, outs[0], outs[1]).astype(o_ref.dtype)


def _fox_prompt(q, kb, vb, ct, batch, seq):
    nq = seq // ATTN_BLOCK
    n_pairs = FOX_WIDTH // LANES
    return pl.pallas_call(
        _fox_prompt_kernel,
        grid=(batch, n_pairs, nq),
        in_specs=[
            pl.BlockSpec((ATTN_BLOCK, LANES), lambda b, p, i: (b * nq + i, p)),
            pl.BlockSpec((seq, LANES), lambda b, p, i: (b, p)),
            pl.BlockSpec((seq, LANES), lambda b, p, i: (b, p)),
            pl.BlockSpec((1, 1, 2, seq), lambda b, p, i: (b, p, 0, 0)),
        ],
        out_specs=pl.BlockSpec((ATTN_BLOCK, LANES), lambda b, p, i: (b * nq + i, p)),
        out_shape=jax.ShapeDtypeStruct((batch * seq, FOX_WIDTH), BF16),
        compiler_params=_params(3),
        name="fox_prompt",
    )(q, kb, vb, ct)


def _decode_bias_kernel(pt_ref, *refs):
    del pt_ref
    n_pages = len(refs) - 6
    page_refs = refs[:n_pages]
    lfnew_ref, within_ref, total_ref, expand_ref, upper_ref, out_ref = refs[n_pages:]
    x = jnp.concatenate([r[0] for r in page_refs], axis=0)
    within = _dot3(x, within_ref[...])
    totals = _dot3(x, total_ref[...])
    later = _dot3_left(upper_ref[...], totals) + lfnew_ref[0]
    out_ref[0] = within + _dot3(later, expand_ref[...])


def _decode_bias(page_table, lf_cache, lf_new):
    bsz, n_pages = page_table.shape
    ph = PAGE_SIZE * N_FOX_HEADS
    src = jnp.arange(ph)
    dst = jnp.arange(ph)
    same_head = (src % N_FOX_HEADS)[:, None] == (dst // PAGE_SIZE)[None, :]
    after = (src // N_FOX_HEADS)[:, None] > (dst % PAGE_SIZE)[None, :]
    within = (same_head & after).astype(BF16)
    total = ((src % N_FOX_HEADS)[:, None] == jnp.arange(LANES)[None, :]).astype(BF16)
    expand = (jnp.arange(LANES)[:, None] == (dst // PAGE_SIZE)[None, :]).astype(BF16)
    upper = (jnp.arange(n_pages)[None, :] > jnp.arange(n_pages)[:, None]).astype(BF16)

    def page_spec(j):
        return pl.BlockSpec((1, 1, ph), lambda b, pt: (pt[b, j], 0, 0))

    const = lambda shape: pl.BlockSpec(shape, lambda b, pt: (0,) * len(shape), pipeline_mode=pl.Buffered(1))
    grid_spec = pltpu.PrefetchScalarGridSpec(
        num_scalar_prefetch=1,
        grid=(bsz,),
        in_specs=[page_spec(j) for j in range(n_pages)] + [
            pl.BlockSpec((1, 1, LANES), lambda b, pt: (b, 0, 0)),
            const(within.shape), const(total.shape), const(expand.shape), const(upper.shape)],
        out_specs=pl.BlockSpec((1, n_pages, ph), lambda b, pt: (b, 0, 0)),
    )
    return pl.pallas_call(
        _decode_bias_kernel,
        grid_spec=grid_spec,
        out_shape=jax.ShapeDtypeStruct((bsz, n_pages, ph), F32),
        compiler_params=_params(1),
        name="decode_bias",
    )(page_table, *([lf_cache] * n_pages), lf_new, within, total, expand, upper)


def _decode_attn_kernel(pt_ref, q_ref, knew_ref, vnew_ref, bias_ref, *refs):
    del pt_ref
    n = PAGES_PER_STEP
    k_refs = refs[:n]
    v_refs = refs[n:2 * n]
    o_ref, m_ref, l_ref, acc_ref = refs[2 * n:]
    g = pl.program_id(1)
    qf = _head_rows(q_ref[0], N_FOX_HEADS, FOX_WIDTH)
    qb = qf.astype(BF16)

    @pl.when(g == 0)
    def _():
        s_new = jnp.sum(qf * knew_ref[0], axis=-1, keepdims=True)
        m_ref[...] = jnp.broadcast_to(s_new, m_ref.shape)
        l_ref[...] = jnp.ones_like(l_ref)
        acc_ref[...] = _head_rows(vnew_ref[0], N_FOX_HEADS, FOX_WIDTH)

    scores = []
    for j in range(n):
        kp = k_refs[j][0].astype(BF16)
        s = lax.dot_general(qb, kp, (((1,), (1,)), ((), ())), preferred_element_type=F32)
        scores.append(s + bias_ref[0, j])
    z = jnp.concatenate(scores, axis=-1)
    m_old = m_ref[:, 0:1]
    m_new = jnp.maximum(m_old, jnp.max(z, axis=-1, keepdims=True))
    alpha = jnp.exp(m_old - m_new)
    p = jnp.exp(z - m_new)
    l_ref[...] = alpha * l_ref[...] + jnp.sum(p, axis=-1, keepdims=True)
    m_ref[...] = jnp.broadcast_to(m_new, m_ref.shape)
    pb = p.astype(BF16)
    acc = alpha * acc_ref[...]
    for j in range(n):
        acc = acc + jnp.dot(pb[:, j * PAGE_SIZE:(j + 1) * PAGE_SIZE], v_refs[j][0].astype(BF16),
                            preferred_element_type=F32)
    acc_ref[...] = acc

    @pl.when(g == pl.num_programs(1) - 1)
    def _():
        o = _head_rows(jnp.ones((1, FOX_WIDTH), F32), N_FOX_HEADS, FOX_WIDTH) * acc_ref[...] / l_ref[:, 0:1]
        o_ref[0] = jnp.sum(o, axis=0, keepdims=True)


def _decode_attn(page_table, q, knew, vnew, bias, k_cache, v_cache):
    bsz, n_pages = page_table.shape
    n = PAGES_PER_STEP
    n_groups = n_pages // n
    per_seq = pl.BlockSpec((1, 1, FOX_WIDTH), lambda b, g, pt: (b, 0, 0))

    def page_spec(j):
        return pl.BlockSpec((1, PAGE_SIZE, FOX_WIDTH), lambda b, g, pt: (pt[b, g * n + j], 0, 0))

    grid_spec = pltpu.PrefetchScalarGridSpec(
        num_scalar_prefetch=1,
        grid=(bsz, n_groups),
        in_specs=[per_seq, per_seq, per_seq,
                  pl.BlockSpec((1, n, N_FOX_HEADS, PAGE_SIZE), lambda b, g, pt: (b, g, 0, 0))]
                 + [page_spec(j) for j in range(n)] * 2,
        out_specs=per_seq,
        scratch_shapes=[pltpu.VMEM((N_FOX_HEADS, LANES), F32), pltpu.VMEM((N_FOX_HEADS, LANES), F32),
                        pltpu.VMEM((N_FOX_HEADS, FOX_WIDTH), F32)],
    )
    return pl.pallas_call(
        _decode_attn_kernel,
        grid_spec=grid_spec,
        out_shape=jax.ShapeDtypeStruct((bsz, 1, FOX_WIDTH), F32),
        compiler_params=_params(2),
        name="decode_attn",
    )(page_table, q, knew, vnew, bias, *([k_cache] * n), *([v_cache] * n))


def _mem_decode_kernel(q_ref, mk_ref, mv_ref, o_ref):
    rows = 8
    qb = _head_rows(q_ref[0], rows, MEM_WIDTH).astype(BF16)
    s = lax.dot_general(qb, mk_ref[0].astype(BF16), (((1,), (1,)), ((), ())), preferred_element_type=F32)
    m = jnp.max(s, axis=-1, keepdims=True)
    p = jnp.exp(s - m)
    l = jnp.sum(p, axis=-1, keepdims=True)
    o = jnp.dot(p.astype(BF16), mv_ref[0].astype(BF16), preferred_element_type=F32) / l
    o = _head_rows(jnp.ones((1, MEM_WIDTH), F32), rows, MEM_WIDTH) * o
    o_ref[0] = jnp.sum(o, axis=0, keepdims=True).astype(o_ref.dtype)


def _mem_decode(qm, mk, mv):
    bsz = qm.shape[0]
    return pl.pallas_call(
        _mem_decode_kernel,
        grid=(bsz,),
        in_specs=[pl.BlockSpec((1, 1, MEM_WIDTH), lambda b: (b, 0, 0)),
                  pl.BlockSpec((1, N_MEM, MEM_WIDTH), lambda b: (b, 0, 0)),
                  pl.BlockSpec((1, N_MEM, MEM_WIDTH), lambda b: (b, 0, 0))],
        out_specs=pl.BlockSpec((1, 1, MEM_WIDTH), lambda b: (b, 0, 0)),
        out_shape=jax.ShapeDtypeStruct((bsz, 1, MEM_WIDTH), BF16),
        compiler_params=_params(1),
        name="mem_decode",
    )(qm, mk, mv)


def _conv_norm_act(conv, gcn_ref, bcn_ref):
    mu = jnp.mean(conv, axis=-1, keepdims=True)
    d = conv - mu
    var = jnp.mean(d * d, axis=-1, keepdims=True)
    y = d * lax.rsqrt(var + EPS) * gcn_ref[...] + bcn_ref[...]
    return y * _sigmoid(y)


def _merge_project(x, yfox, yb, yc, g_ref, wa_ref, wb_ref, wc_ref, wo_ref, gffn_ref, x1_ref, h2_ref):
    ga = g_ref[:, :D_MODEL].astype(F32)
    gb = g_ref[:, D_MODEL:2 * D_MODEL].astype(F32)
    gc = g_ref[:, 2 * D_MODEL:].astype(F32)
    merged = (ga * jnp.dot(yfox, wa_ref[...], preferred_element_type=F32)
              + gb * jnp.dot(yb.astype(BF16), wb_ref[...], preferred_element_type=F32)
              + gc * jnp.dot(yc, wc_ref[...], preferred_element_type=F32))
    x1 = x + jnp.dot(merged.astype(BF16), wo_ref[...], preferred_element_type=F32)
    x1_ref[...] = x1
    h2_ref[...] = (_rms_scale(x1) * gffn_ref[...]).astype(BF16)


def _tail_prompt_kernel(x_ref, yfox_ref, u_ref, uhalo_ref, qm_ref, g_ref, mk_ref, mv_ref,
                        wdw_ref, gcn_ref, bcn_ref, wa_ref, wb_ref, wc_ref, wo_ref, gffn_ref,
                        x1_ref, h2_ref, ubuf_ref, *, tiles_per_seq):
    i = pl.program_id(0)
    tm = x_ref.shape[0]
    halo = uhalo_ref[...]
    ubuf_ref[0:CONV_HALO, :] = jnp.where(i % tiles_per_seq == 0, jnp.zeros_like(halo), halo)
    ubuf_ref[CONV_HALO:, :] = u_ref[...]
    base = CONV_HALO - (CONV_WIDTH - 1)
    chunk = 64
    convs = []
    for r0 in range(0, tm, chunk):
        acc = jnp.zeros((chunk, CONV_CH), F32)
        for j in range(CONV_WIDTH):
            acc = acc + wdw_ref[j:j + 1, :] * ubuf_ref[base + r0 + j:base + r0 + j + chunk, :]
        convs.append(acc)
    yb = _conv_norm_act(jnp.concatenate(convs, axis=0), gcn_ref, bcn_ref)

    qm = qm_ref[...]
    lane = lax.broadcasted_iota(jnp.int32, (tm, MEM_WIDTH), 1)
    mk = mk_ref[...]
    mv = mv_ref[...]
    yc = jnp.zeros((tm, MEM_WIDTH), F32)
    for h in range(N_MEM_HEADS):
        in_head = (lane >= h * HEAD_DIM) & (lane < (h + 1) * HEAD_DIM)
        qh = jnp.where(in_head, qm, jnp.zeros((), qm.dtype))
        s = lax.dot_general(qh, mk, (((1,), (1,)), ((), ())), preferred_element_type=F32)
        p = jnp.exp(s - jnp.max(s, axis=-1, keepdims=True))
        o = jnp.dot(p.astype(BF16), mv, preferred_element_type=F32) / jnp.sum(p, axis=-1, keepdims=True)
        yc = jnp.where(in_head, o, yc)

    _merge_project(x_ref[...], yfox_ref[...], yb, yc.astype(BF16), g_ref,
                   wa_ref, wb_ref, wc_ref, wo_ref, gffn_ref, x1_ref, h2_ref)


def _tail_sample_kernel(x_ref, yfox_ref, u_ref, state_ref, yc_ref, g_ref,
                        wdw_ref, gcn_ref, bcn_ref, wa_ref, wb_ref, wc_ref, wo_ref, gffn_ref,
                        x1_ref, h2_ref):
    conv = wdw_ref[CONV_WIDTH - 1:CONV_WIDTH, :] * u_ref[...]
    for j in range(CONV_WIDTH - 1):
        conv = conv + wdw_ref[j:j + 1, :] * state_ref[j]
    yb = _conv_norm_act(conv, gcn_ref, bcn_ref)
    _merge_project(x_ref[...], yfox_ref[...], yb, yc_ref[...], g_ref,
                   wa_ref, wb_ref, wc_ref, wo_ref, gffn_ref, x1_ref, h2_ref)


def _tail_weights(wdw, gcn, bcn, wa, wb, wc, wo, gffn):
    ws = (wdw, gcn, bcn, wa, wb, wc, wo, gffn)
    return ws, [_resident(w.shape) for w in ws]


def _tail_prompt(x2d, yfox, u, qm, g, mkb, mvb, tiles_per_seq, tm, *weights):
    n = x2d.shape[0]
    ws, wspecs = _tail_weights(*weights)
    rows = lambda w: pl.BlockSpec((tm, w), lambda i: (i, 0))
    halo_blocks = tm // CONV_HALO
    return pl.pallas_call(
        functools.partial(_tail_prompt_kernel, tiles_per_seq=tiles_per_seq),
        grid=(n // tm,),
        in_specs=[rows(D_MODEL), rows(FOX_WIDTH), rows(CONV_CH),
                  pl.BlockSpec((CONV_HALO, CONV_CH), lambda i: (jnp.maximum(i * halo_blocks - 1, 0), 0)),
                  rows(MEM_WIDTH), rows(3 * D_MODEL),
                  pl.BlockSpec((N_MEM, MEM_WIDTH), lambda i: (i // tiles_per_seq, 0)),
                  pl.BlockSpec((N_MEM, MEM_WIDTH), lambda i: (i // tiles_per_seq, 0))] + wspecs,
        out_specs=[rows(D_MODEL), rows(D_MODEL)],
        out_shape=[jax.ShapeDtypeStruct((n, D_MODEL), F32), jax.ShapeDtypeStruct((n, D_MODEL), BF16)],
        scratch_shapes=[pltpu.VMEM((CONV_HALO + tm, CONV_CH), F32)],
        compiler_params=_params(1),
        name="tail_prompt",
    )(x2d, yfox, u, u, qm, g, mkb, mvb, *ws)


def _tail_sample(x2d, yfox, u, state_t, yc, g, *weights):
    n = x2d.shape[0]
    ws, wspecs = _tail_weights(*weights)
    full = lambda a: pl.BlockSpec(a.shape, lambda i: (0,) * a.ndim)
    ins = (x2d, yfox, u, state_t, yc, g)
    return pl.pallas_call(
        _tail_sample_kernel,
        grid=(1,),
        in_specs=[full(a) for a in ins] + wspecs,
        out_specs=[pl.BlockSpec((n, D_MODEL), lambda i: (0, 0))] * 2,
        out_shape=[jax.ShapeDtypeStruct((n, D_MODEL), F32), jax.ShapeDtypeStruct((n, D_MODEL), BF16)],
        compiler_params=_params(1),
        name="tail_sample",
    )(*ins, *ws)


def _ffn_finish(x1_ref, acc_ref, gfin_ref, y_ref):
    x2 = x1_ref[...] + acc_ref[...]
    y_ref[...] = _rms_scale(x2) * gfin_ref[...]


def _ffn_prompt_kernel(h2_ref, halo_ref, x1_ref, wg_ref, wu_ref, wd_ref, wcv_ref, gfin_ref,
                       y_ref, gtail_ref, hcat_ref, gbuf_ref, acc_ref, *, tiles_per_seq):
    i = pl.program_id(0)
    tm = h2_ref.shape[0]
    halo = halo_ref[...]
    hcat_ref[0:FFN_HALO, :] = jnp.where(i % tiles_per_seq == 0, jnp.zeros_like(halo), halo)
    hcat_ref[FFN_HALO:, :] = h2_ref[...]
    acc_ref[...] = jnp.zeros_like(acc_ref)

    def chunk(c, carry):
        gbuf_ref[...] = jnp.dot(hcat_ref[...], wg_ref[c], preferred_element_type=F32)
        up = jnp.dot(h2_ref[...], wu_ref[c], preferred_element_type=F32)
        w = wcv_ref[c]
        cv = (w[0:1, :] * gbuf_ref[FFN_HALO - 2:FFN_HALO - 2 + tm, :]
              + w[1:2, :] * gbuf_ref[FFN_HALO - 1:FFN_HALO - 1 + tm, :]
              + w[2:3, :] * gbuf_ref[FFN_HALO:FFN_HALO + tm, :])
        act = (cv * _sigmoid(cv) * up).astype(BF16)
        acc_ref[...] += jnp.dot(act, wd_ref[c], preferred_element_type=F32)
        gtail_ref[0, c] = gbuf_ref[FFN_HALO + tm - 8:FFN_HALO + tm, :]
        return carry

    lax.fori_loop(0, wg_ref.shape[0], chunk, 0)
    _ffn_finish(x1_ref, acc_ref, gfin_ref, y_ref)


def _ffn_sample_kernel(h2_ref, state_ref, x1_ref, wg_ref, wu_ref, wd_ref, wcv_ref, gfin_ref,
                       y_ref, gt_ref, acc_ref):
    acc_ref[...] = jnp.zeros_like(acc_ref)

    def chunk(c, carry):
        h2 = h2_ref[...]
        gt = jnp.dot(h2, wg_ref[c], preferred_element_type=F32)
        up = jnp.dot(h2, wu_ref[c], preferred_element_type=F32)
        w = wcv_ref[c]
        cv = w[0:1, :] * state_ref[c, 0] + w[1:2, :] * state_ref[c, 1] + w[2:3, :] * gt
        act = (cv * _sigmoid(cv) * up).astype(BF16)
        acc_ref[...] += jnp.dot(act, wd_ref[c], preferred_element_type=F32)
        gt_ref[c] = gt
        return carry

    lax.fori_loop(0, wg_ref.shape[0], chunk, 0)
    _ffn_finish(x1_ref, acc_ref, gfin_ref, y_ref)


def _ffn_prompt(h2, x1, tiles_per_seq, tm, wg, wu, wd, wcv, gfin):
    n = h2.shape[0]
    nt = n // tm
    n_chunks = wg.shape[0]
    rows = lambda w: pl.BlockSpec((tm, w), lambda i: (i, 0))
    halo_blocks = tm // FFN_HALO
    return pl.pallas_call(
        functools.partial(_ffn_prompt_kernel, tiles_per_seq=tiles_per_seq),
        grid=(nt,),
        in_specs=[rows(D_MODEL),
                  pl.BlockSpec((FFN_HALO, D_MODEL), lambda i: (jnp.maximum(i * halo_blocks - 1, 0), 0)),
                  rows(D_MODEL), _resident(wg.shape), _resident(wu.shape), _resident(wd.shape),
                  _resident(wcv.shape), _resident(gfin.shape)],
        out_specs=[rows(D_MODEL), pl.BlockSpec((1, n_chunks, 8, FF_CHUNK), lambda i: (i, 0, 0, 0))],
        out_shape=[jax.ShapeDtypeStruct((n, D_MODEL), F32),
                   jax.ShapeDtypeStruct((nt, n_chunks, 8, FF_CHUNK), F32)],
        scratch_shapes=[pltpu.VMEM((FFN_HALO + tm, D_MODEL), BF16),
                        pltpu.VMEM((FFN_HALO + tm, FF_CHUNK), F32),
                        pltpu.VMEM((tm, D_MODEL), F32)],
        compiler_params=_params(1),
        name="ffn_prompt",
    )(h2, h2, x1, wg, wu, wd, wcv, gfin)


def _ffn_sample(h2, state_c, x1, wg, wu, wd, wcv, gfin):
    n = h2.shape[0]
    n_chunks = wg.shape[0]
    full = lambda a: pl.BlockSpec(a.shape, lambda i: (0,) * a.ndim)
    return pl.pallas_call(
        _ffn_sample_kernel,
        grid=(1,),
        in_specs=[full(h2), full(state_c), full(x1), _resident(wg.shape), _resident(wu.shape),
                  _resident(wd.shape), _resident(wcv.shape), _resident(gfin.shape)],
        out_specs=[pl.BlockSpec((n, D_MODEL), lambda i: (0, 0)),
                   pl.BlockSpec((n_chunks, n, FF_CHUNK), lambda i: (0, 0, 0))],
        out_shape=[jax.ShapeDtypeStruct((n, D_MODEL), F32),
                   jax.ShapeDtypeStruct((n_chunks, n, FF_CHUNK), F32)],
        scratch_shapes=[pltpu.VMEM((n, D_MODEL), F32)],
        compiler_params=_params(1),
        name="ffn_sample",
    )(h2, state_c, x1, wg, wu, wd, wcv, gfin)


def _pad_rows(a, rows):
    return jnp.concatenate([a, jnp.zeros((rows - a.shape[0],) + a.shape[1:], a.dtype)], axis=0)


def kernel(x_prompt, x_sample, cache_fox_k, cache_fox_v, cache_fox_logf, state_conv, state_ffn_conv, cache_mem_k, cache_mem_v, page_table, mem_prompt, g_mix, w_in, b_f, w_conv_dw, g_conv_norm, b_conv_norm, g_mem, w_mem_kv, w_a, w_b, w_c, w_o, g_ffn, w_ffn_in, w_ffn_conv, w_down, g_final):
    depth = g_mix.shape[0]
    assert depth == 1, "kernel is written for the single-layer trunk the problem states"
    bp, seq, _ = x_prompt.shape
    bs, dec_seq, _ = x_sample.shape
    assert dec_seq == 1 and seq % ROW_TILE == 0 and bs % 8 == 0
    n_pool = cache_fox_k.shape[1]
    n_pages = page_table.shape[1]
    assert n_pages % PAGES_PER_STEP == 0
    l = 0

    w = w_in[l]
    o_f = 3 * FOX_WIDTH
    o_glu = o_f + N_FOX_HEADS
    o_qm = o_glu + 2 * CONV_CH
    o_gate = o_qm + MEM_WIDTH
    wqkv = w[:, :o_f].astype(BF16)
    wf = jnp.pad(w[:, o_f:o_glu], ((0, 0), (0, LANES - N_FOX_HEADS))).astype(BF16)
    bfp = jnp.pad(b_f[l][None, :], ((0, 0), (0, LANES - N_FOX_HEADS)))
    wglu = w[:, o_glu:o_qm].astype(BF16)
    wqm = w[:, o_qm:o_gate].astype(BF16)
    wgate = w[:, o_gate:].astype(BF16)
    gmix = g_mix[l][None, :]
    front_w = (gmix, wqkv, wf, bfp, wglu, wqm, wgate)

    tail_w = (_pad_rows(w_conv_dw[l], CONV_HALO), g_conv_norm[l][None, :], b_conv_norm[l][None, :],
              w_a[l].astype(BF16), w_b[l].astype(BF16), w_c[l].astype(BF16), w_o[l].astype(BF16),
              g_ffn[l][None, :])

    n_chunks = D_FF // FF_CHUNK
    wfi = w_ffn_in[l]
    wg = wfi[:, :D_FF].reshape(D_MODEL, n_chunks, FF_CHUNK).transpose(1, 0, 2).astype(BF16)
    wu = wfi[:, D_FF:].reshape(D_MODEL, n_chunks, FF_CHUNK).transpose(1, 0, 2).astype(BF16)
    wd = w_down[l].reshape(n_chunks, FF_CHUNK, D_MODEL).astype(BF16)
    wcv = _pad_rows(w_ffn_conv[l], 8).reshape(8, n_chunks, FF_CHUNK).transpose(1, 0, 2)
    ffn_w = (wg, wu, wd, wcv, g_final[None, :])

    tiles_per_seq = seq // ROW_TILE
    xp2 = x_prompt.reshape(bp * seq, D_MODEL)
    q_p, k_p, v_p, kb_p, vb_p, lf_p, c_p, u_p, qm_p, g_p = _front(xp2, tiles_per_seq, ROW_TILE, *front_w)
    mk_p, mv_p, mkb_p, mvb_p = _memkv(mem_prompt.reshape(bp * N_MEM, D_MODEL), g_mem[l][None, :],
                                      w_mem_kv[l].astype(BF16))
    ct = c_p[:, :N_FOX_HEADS].reshape(bp, seq, FOX_WIDTH // LANES, 2).transpose(0, 2, 3, 1)
    yfox_p = _fox_prompt(q_p, kb_p, vb_p, ct, bp, seq)
    x1_p, h2_p = _tail_prompt(xp2, yfox_p, u_p, qm_p, g_p, mkb_p, mvb_p, tiles_per_seq, ROW_TILE, *tail_w)
    y_p, gtail_p = _ffn_prompt(h2_p, x1_p, tiles_per_seq, ROW_TILE, *ffn_w)

    xs2 = x_sample.reshape(bs, D_MODEL)
    q_s, k_s, v_s, _, _, lf_s, _, u_s, qm_s, g_s = _front(xs2, 1, bs, *front_w)
    ph = PAGE_SIZE * N_FOX_HEADS
    bias = _decode_bias(page_table, cache_fox_logf[l].reshape(n_pool, 1, ph), lf_s.reshape(bs, 1, LANES))
    yfox_s = _decode_attn(page_table, q_s.reshape(bs, 1, FOX_WIDTH), k_s.reshape(bs, 1, FOX_WIDTH),
                          v_s.reshape(bs, 1, FOX_WIDTH), bias.reshape(bs, n_pages, N_FOX_HEADS, PAGE_SIZE),
                          cache_fox_k[l].reshape(n_pool, PAGE_SIZE, FOX_WIDTH),
                          cache_fox_v[l].reshape(n_pool, PAGE_SIZE, FOX_WIDTH))
    yc_s = _mem_decode(qm_s.reshape(bs, 1, MEM_WIDTH), cache_mem_k[l].reshape(bs, N_MEM, MEM_WIDTH),
                       cache_mem_v[l].reshape(bs, N_MEM, MEM_WIDTH))
    x1_s, h2_s = _tail_sample(xs2, yfox_s.reshape(bs, FOX_WIDTH).astype(BF16), u_s,
                              state_conv[l].transpose(1, 0, 2), yc_s.reshape(bs, MEM_WIDTH), g_s, *tail_w)
    ffn_state_c = state_ffn_conv[l].reshape(bs, FFN_CONV_WIDTH - 1, n_chunks, FF_CHUNK).transpose(2, 1, 0, 3)
    y_s, gt_s = _ffn_sample(h2_s, ffn_state_c, x1_s, *ffn_w)

    y_prompt = y_p.reshape(bp, seq, D_MODEL)
    y_sample = y_s.reshape(bs, 1, D_MODEL)
    fox_k_p = k_p.reshape(1, bp, seq, N_FOX_HEADS, HEAD_DIM)
    fox_v_p = v_p.reshape(1, bp, seq, N_FOX_HEADS, HEAD_DIM)
    fox_logf_p = lf_p[:, :N_FOX_HEADS].reshape(1, bp, seq, N_FOX_HEADS)
    conv_state_p = u_p.reshape(bp, seq, CONV_CH)[:, seq - (CONV_WIDTH - 1):][None]
    gt_last = gtail_p.reshape(bp, tiles_per_seq, n_chunks, 8, FF_CHUNK)[:, -1, :, 8 - (FFN_CONV_WIDTH - 1):, :]
    ffn_state_p = gt_last.transpose(0, 2, 1, 3).reshape(1, bp, FFN_CONV_WIDTH - 1, D_FF)
    mem_k_p = mk_p.reshape(1, bp, N_MEM, N_MEM_HEADS, HEAD_DIM)
    mem_v_p = mv_p.reshape(1, bp, N_MEM, N_MEM_HEADS, HEAD_DIM)
    fox_k_s = k_s.reshape(1, bs, 1, N_FOX_HEADS, HEAD_DIM)
    fox_v_s = v_s.reshape(1, bs, 1, N_FOX_HEADS, HEAD_DIM)
    fox_logf_s = lf_s[:, :N_FOX_HEADS].reshape(1, bs, 1, N_FOX_HEADS)
    conv_state_s = jnp.concatenate([state_conv[l][:, 1:], u_s[:, None, :]], axis=1)[None]
    gt_new = gt_s.transpose(1, 0, 2).reshape(bs, 1, D_FF)
    ffn_state_s = jnp.concatenate([state_ffn_conv[l][:, 1:], gt_new], axis=1)[None]
    return (y_prompt, y_sample, fox_k_p, fox_v_p, fox_logf_p, conv_state_p, ffn_state_p, mem_k_p, mem_v_p,
            fox_k_s, fox_v_s, fox_logf_s, conv_state_s, ffn_state_s)
```

```python
import functools

import jax
import jax.numpy as jnp
from jax import lax
from jax.experimental import pallas as pl
from jax.experimental.pallas import tpu as pltpu

F32 = jnp.float32
BF16 = jnp.bfloat16

D_MODEL = 1024
HEAD_DIM = 64
N_FOX_HEADS = 8
FOX_WIDTH = N_FOX_HEADS * HEAD_DIM
CONV_CH = 256
CONV_WIDTH = 31
N_MEM = 256
N_MEM_HEADS = 4
MEM_WIDTH = N_MEM_HEADS * HEAD_DIM
D_FF = 3 * D_MODEL
FFN_CONV_WIDTH = 3
PAGE_SIZE = 128
EPS = 1e-6
ATTN_SCALE = HEAD_DIM ** -0.5
NEG_INF = -1e30

LANES = 128
SUBLANES = 8
BF16_SUBLANES = 16
ROW_TILE = 256
FFN_ROW_TILE = 512
ATTN_BLOCK = 512
CONV_HALO = 32
FFN_HALO = BF16_SUBLANES
FF_CHUNK = 512
PAGES_PER_STEP = 8
VMEM_LIMIT_BYTES = 56 * 1024 * 1024

_NT = (((1,), (1,)), ((), ()))


def _params(n_axes):
    return pltpu.CompilerParams(dimension_semantics=("arbitrary",) * n_axes,
                                vmem_limit_bytes=VMEM_LIMIT_BYTES)


def _resident(shape):
    nd = len(shape)
    return pl.BlockSpec(shape, lambda *_: (0,) * nd, pipeline_mode=pl.Buffered(1))


def _rms_scale(x):
    return x * lax.rsqrt(jnp.mean(x * x, axis=-1, keepdims=True) + EPS)


def _sigmoid(x):
    return 1.0 / (1.0 + jnp.exp(-x))


def _log_sigmoid(x):
    return jnp.minimum(x, 0.0) - jnp.log(1.0 + jnp.exp(-jnp.abs(x)))


def _split3(x):
    hi = x.astype(BF16)
    r1 = x - hi.astype(F32)
    mid = r1.astype(BF16)
    lo = (r1 - mid.astype(F32)).astype(BF16)
    return hi, mid, lo


def _dot3(a, b01):
    return sum(jnp.dot(t, b01, preferred_element_type=F32) for t in _split3(a))


def _dot3_left(a01, b):
    return sum(jnp.dot(a01, t, preferred_element_type=F32) for t in _split3(b))


def _dot3_nt(a01, b):
    return sum(lax.dot_general(a01, t, _NT, preferred_element_type=F32) for t in _split3(b))


def _ones_where(cond):
    return jnp.where(cond, 1.0, 0.0).astype(BF16)


def _head_mask(n_rows, width):
    r = lax.broadcasted_iota(jnp.int32, (n_rows, width), 0)
    c = lax.broadcasted_iota(jnp.int32, (n_rows, width), 1)
    return (c >= r * HEAD_DIM) & (c < (r + 1) * HEAD_DIM)


def _head_rows(row, n_rows, width):
    return jnp.where(_head_mask(n_rows, width), jnp.broadcast_to(row.astype(F32), (n_rows, width)), 0.0)


def _heads_to_row(col, width):
    n_rows = col.shape[0]
    spread = jnp.where(_head_mask(n_rows, width), jnp.broadcast_to(col[:, 0:1], (n_rows, width)), 0.0)
    return jnp.sum(spread, axis=0, keepdims=True)


def _front_kernel(x_ref, gmix_ref, wq_ref, wkvt_ref, wv_ref, wft_ref, bf_ref, wglu_ref, wqm_ref, wgate_ref,
                  q_ref, kt_ref, ktb_ref, vt_ref, vb_ref, lft_ref, ct_ref, u_ref, qm_ref, g_ref, *rest,
                  tiles_per_seq, emit_rows):
    if emit_rows:
        krow_ref, vrow_ref, carry_ref = rest
    else:
        (carry_ref,) = rest
    i = pl.program_id(0)
    tm = x_ref.shape[0]
    h = (_rms_scale(x_ref[...]) * gmix_ref[...]).astype(BF16)

    q_ref[...] = (jnp.dot(h, wq_ref[...], preferred_element_type=F32) * ATTN_SCALE).astype(BF16)
    kvt = lax.dot_general(wkvt_ref[...], h, _NT, preferred_element_type=F32)
    kt = kvt[:FOX_WIDTH]
    kt_ref[0] = kt
    ktb_ref[0] = kt.astype(BF16)
    vt_ref[0] = kvt[FOX_WIDTH:]
    v = jnp.dot(h, wv_ref[...], preferred_element_type=F32)
    vb_ref[...] = v.astype(BF16)
    if emit_rows:
        vrow_ref[...] = v
        krow_ref[...] = lax.dot_general(h, wkvt_ref[0:FOX_WIDTH, :], _NT, preferred_element_type=F32)

    flt = lax.dot_general(wft_ref[...], h, _NT, preferred_element_type=F32)[:N_FOX_HEADS] + bf_ref[...]
    lft = _log_sigmoid(flt)
    lft_ref[0] = lft

    @pl.when(i % tiles_per_seq == 0)
    def _():
        carry_ref[...] = jnp.zeros_like(carry_ref)

    r = lax.broadcasted_iota(jnp.int32, (tm, tm), 0)
    c = lax.broadcasted_iota(jnp.int32, (tm, tm), 1)
    csum = _dot3(lft, _ones_where(r <= c)) + carry_ref[:, 0:1]
    ct_ref[0] = csum
    carry_ref[...] = jnp.broadcast_to(csum[:, tm - 1:tm], carry_ref.shape)

    glu = jnp.dot(h, wglu_ref[...], preferred_element_type=F32)
    u_ref[...] = glu[:, :CONV_CH] * _sigmoid(glu[:, CONV_CH:])
    qm = jnp.dot(h, wqm_ref[...], preferred_element_type=F32)
    qm_ref[...] = (qm * ATTN_SCALE).astype(BF16)
    g_ref[...] = _sigmoid(jnp.dot(h, wgate_ref[...], preferred_element_type=F32)).astype(BF16)


def _front(x2d, batch, tm, emit_rows, *weights):
    n = x2d.shape[0]
    seq = n // batch
    tiles_per_seq = seq // tm
    rows = lambda w: pl.BlockSpec((tm, w), lambda i: (i, 0))
    cols = lambda r: pl.BlockSpec((1, r, tm), lambda i: (i // tiles_per_seq, 0, i % tiles_per_seq))
    out = [
        (jax.ShapeDtypeStruct((n, FOX_WIDTH), BF16), rows(FOX_WIDTH)),
        (jax.ShapeDtypeStruct((batch, FOX_WIDTH, seq), F32), cols(FOX_WIDTH)),
        (jax.ShapeDtypeStruct((batch, FOX_WIDTH, seq), BF16), cols(FOX_WIDTH)),
        (jax.ShapeDtypeStruct((batch, FOX_WIDTH, seq), F32), cols(FOX_WIDTH)),
        (jax.ShapeDtypeStruct((n, FOX_WIDTH), BF16), rows(FOX_WIDTH)),
        (jax.ShapeDtypeStruct((batch, N_FOX_HEADS, seq), F32), cols(N_FOX_HEADS)),
        (jax.ShapeDtypeStruct((batch, N_FOX_HEADS, seq), F32), cols(N_FOX_HEADS)),
        (jax.ShapeDtypeStruct((n, CONV_CH), F32), rows(CONV_CH)),
        (jax.ShapeDtypeStruct((n, MEM_WIDTH), BF16), rows(MEM_WIDTH)),
        (jax.ShapeDtypeStruct((n, 3 * D_MODEL), BF16), rows(3 * D_MODEL)),
    ]
    if emit_rows:
        out += [(jax.ShapeDtypeStruct((n, FOX_WIDTH), F32), rows(FOX_WIDTH))] * 2
    return pl.pallas_call(
        functools.partial(_front_kernel, tiles_per_seq=tiles_per_seq, emit_rows=emit_rows),
        grid=(n // tm,),
        in_specs=[rows(D_MODEL)] + [_resident(w.shape) for w in weights],
        out_specs=[s for _, s in out],
        out_shape=[s for s, _ in out],
        scratch_shapes=[pltpu.VMEM((N_FOX_HEADS, LANES), F32)],
        compiler_params=_params(1),
        name="front_rows" if emit_rows else "front",
    )(x2d, *weights)


def _memkv_kernel(m_ref, g_ref, wt_ref, wv_ref, mkt_ref, mvt_ref, mktb_ref, mvb_ref):
    h = (_rms_scale(m_ref[...]) * g_ref[...]).astype(BF16)
    kvt = lax.dot_general(wt_ref[...], h, _NT, preferred_element_type=F32)
    mkt_ref[0] = kvt[:MEM_WIDTH]
    mvt_ref[0] = kvt[MEM_WIDTH:]
    mktb_ref[0] = kvt[:MEM_WIDTH].astype(BF16)
    mvb_ref[...] = jnp.dot(h, wv_ref[...], preferred_element_type=F32).astype(BF16)


def _memkv(mem2d, batch, gmem, wkvt, wv):
    rows = lambda w: pl.BlockSpec((N_MEM, w), lambda i: (i, 0))
    per_b = pl.BlockSpec((1, MEM_WIDTH, N_MEM), lambda i: (i, 0, 0))
    t_shape = (batch, MEM_WIDTH, N_MEM)
    return pl.pallas_call(
        _memkv_kernel,
        grid=(batch,),
        in_specs=[rows(D_MODEL), _resident(gmem.shape), _resident(wkvt.shape), _resident(wv.shape)],
        out_specs=[per_b, per_b, per_b, rows(MEM_WIDTH)],
        out_shape=[jax.ShapeDtypeStruct(t_shape, F32), jax.ShapeDtypeStruct(t_shape, F32),
                   jax.ShapeDtypeStruct(t_shape, BF16), jax.ShapeDtypeStruct((batch * N_MEM, MEM_WIDTH), BF16)],
        compiler_params=_params(1),
        name="memkv",
    )(mem2d, gmem, wkvt, wv)


def _fox_prompt_kernel(q_ref, kt_ref, v_ref, ct_ref, o_ref):
    i = pl.program_id(2)
    tq = q_ref.shape[0]
    tk = tq
    q = q_ref[...]
    lane = lax.broadcasted_iota(jnp.int32, (tq, LANES), 1)
    zero = jnp.zeros((), q.dtype)
    q2 = jnp.concatenate([jnp.where(lane < HEAD_DIM, q, zero), jnp.where(lane >= HEAD_DIM, q, zero)], axis=0)

    def step(j, carry, masked):
        m, l, acc = carry
        start = pl.multiple_of(j * tk, tk)
        s = jnp.dot(q2, kt_ref[0, :, pl.ds(start, tk)], preferred_element_type=F32)
        ck = ct_ref[0, 0, :, pl.ds(start, tk)]
        z = jnp.concatenate([s[:tq] - ck[0:1], s[tq:] - ck[1:2]], axis=0)
        if masked:
            row = lax.broadcasted_iota(jnp.int32, (2 * tq, tk), 0)
            col = lax.broadcasted_iota(jnp.int32, (2 * tq, tk), 1)
            z = jnp.where(col <= (row & (tq - 1)), z, NEG_INF)
        m_new = jnp.maximum(m, jnp.max(z, axis=-1, keepdims=True))
        alpha = jnp.exp(m - m_new)
        p = jnp.exp(z - m_new)
        l = alpha * l + jnp.sum(p, axis=-1, keepdims=True)
        acc = alpha * acc + jnp.dot(p.astype(BF16), v_ref[pl.ds(start, tk), :], preferred_element_type=F32)
        return m_new, l, acc

    init = (jnp.full((2 * tq, 1), NEG_INF, F32), jnp.zeros((2 * tq, 1), F32), jnp.zeros((2 * tq, LANES), F32))
    carry = lax.fori_loop(0, i, functools.partial(step, masked=False), init)
    _, l, acc = step(i, carry, True)
    o = acc / l
    o_ref[...] = jnp.where(lane < HEAD_DIM, o[:tq], o[tq:]).astype(o_ref.dtype)


def _fox_prompt(q, ktb, vb, ct, batch, seq):
    assert ATTN_BLOCK & (ATTN_BLOCK - 1) == 0
    nq = seq // ATTN_BLOCK
    n_pairs = FOX_WIDTH // LANES
    return pl.pallas_call(
        _fox_prompt_kernel,
        grid=(batch, n_pairs, nq),
        in_specs=[
            pl.BlockSpec((ATTN_BLOCK, LANES), lambda b, p, i: (b * nq + i, p)),
            pl.BlockSpec((1, LANES, seq), lambda b, p, i: (b, p, 0)),
            pl.BlockSpec((seq, LANES), lambda b, p, i: (b, p)),
            pl.BlockSpec((1, 1, 2, seq), lambda b, p, i: (b, p, 0, 0)),
        ],
        out_specs=pl.BlockSpec((ATTN_BLOCK, LANES), lambda b, p, i: (b * nq + i, p)),
        out_shape=jax.ShapeDtypeStruct((batch * seq, FOX_WIDTH), BF16),
        compiler_params=_params(3),
        name="fox_prompt",
    )(q, ktb, vb, ct)


def _page_copies(pt_ref, lf_hbm, k_hbm, v_hbm, lfbuf, kbuf, vbuf, sem, step, slot, n_groups):
    n = PAGES_PER_STEP
    if step is not None:
        seq_i = step // n_groups
        first = (n_groups - 1 - step % n_groups) * n
    copies = []
    for j in range(n):
        page = 0 if step is None else pt_ref[seq_i, first + j]
        copies.append(pltpu.make_async_copy(lf_hbm.at[page], lfbuf.at[slot, j], sem.at[slot, 0, j]))
        copies.append(pltpu.make_async_copy(k_hbm.at[page], kbuf.at[slot, j], sem.at[slot, 1, j]))
        copies.append(pltpu.make_async_copy(v_hbm.at[page], vbuf.at[slot, j], sem.at[slot, 2, j]))
    return copies


def _decode_attn_kernel(pt_ref, q_ref, knew_ref, vnew_ref, lfnew_ref, tw_ref, lf_hbm, k_hbm, v_hbm,
                        o_ref, lfbuf, kbuf, vbuf, sem, m_ref, l_ref, cn_ref, carry_ref, acc_ref):
    n = PAGES_PER_STEP
    g = pl.program_id(1)
    n_groups = pl.num_programs(1)
    step = pl.program_id(0) * n_groups + g
    slot = step % 2
    copies = functools.partial(_page_copies, pt_ref, lf_hbm, k_hbm, v_hbm, lfbuf, kbuf, vbuf, sem,
                               n_groups=n_groups)

    @pl.when(step == 0)
    def _():
        for cp in copies(0, 0):
            cp.start()

    @pl.when(step + 1 < pl.num_programs(0) * n_groups)
    def _():
        for cp in copies(step + 1, 1 - slot):
            cp.start()

    for cp in copies(None, slot):
        cp.wait()

    qf = _head_rows(q_ref[0], N_FOX_HEADS, FOX_WIDTH)
    qb = qf.astype(BF16)

    @pl.when(g == 0)
    def _():
        s_new = jnp.sum(qf * knew_ref[0], axis=-1, keepdims=True)
        m_ref[...] = jnp.broadcast_to(s_new, m_ref.shape)
        l_ref[...] = jnp.ones_like(l_ref)
        cn_ref[...] = jnp.ones_like(cn_ref)
        carry_ref[...] = jnp.broadcast_to(lfnew_ref[0], carry_ref.shape)
        acc_ref[...] = jnp.zeros_like(acc_ref)

    x = lfbuf[slot].reshape(n * N_FOX_HEADS, PAGE_SIZE)
    wt = _dot3(x, tw_ref[...])
    later = carry_ref[...]
    scores = [None] * n
    for j in reversed(range(n)):
        rows = slice(j * N_FOX_HEADS, (j + 1) * N_FOX_HEADS)
        bias = wt[rows, :PAGE_SIZE] + later
        later = later + wt[rows, PAGE_SIZE:]
        scores[j] = jnp.dot(qb, kbuf[slot, j].astype(BF16), preferred_element_type=F32) + bias
    carry_ref[...] = later
    z = jnp.concatenate(scores, axis=-1)
    m_old = m_ref[...]
    m_new = jnp.maximum(m_old, jnp.max(z, axis=-1, keepdims=True))
    alpha = jnp.exp(m_old - m_new)
    p = jnp.exp(z - m_new[:, 0:1])
    l_ref[...] = alpha * l_ref[...] + jnp.sum(p, axis=-1, keepdims=True)
    cn_ref[...] = alpha * cn_ref[...]
    m_ref[...] = m_new
    for h in range(N_FOX_HEADS):
        a = acc_ref[h] * alpha[h:h + 1, :]
        for j in range(n):
            a = a + vbuf[slot, j, h] * p[h:h + 1, j * PAGE_SIZE:(j + 1) * PAGE_SIZE]
        acc_ref[h] = a

    @pl.when(g == n_groups - 1)
    def _():
        acc2d = acc_ref[...].reshape(FOX_WIDTH, PAGE_SIZE)
        ones = jnp.ones((BF16_SUBLANES, PAGE_SIZE), BF16)
        past = _dot3_nt(ones, acc2d)[0:1]
        new = _heads_to_row(cn_ref[...], FOX_WIDTH) * vnew_ref[0]
        o_ref[0] = (past + new) / _heads_to_row(l_ref[...], FOX_WIDTH)


def _decode_attn(page_table, q, knew, vnew, lfnew, lf_cache, k_cache, v_cache):
    bsz, n_pages = page_table.shape
    n = PAGES_PER_STEP
    n_groups = n_pages // n
    pos = jnp.arange(PAGE_SIZE)
    tw = jnp.concatenate([_ones_where(pos[:, None] > pos[None, :]),
                          jnp.ones((PAGE_SIZE, PAGE_SIZE), BF16)], axis=1)

    per_seq = pl.BlockSpec((1, 1, FOX_WIDTH), lambda b, g, pt: (b, 0, 0))
    hbm = pl.BlockSpec(memory_space=pl.ANY)
    grid_spec = pltpu.PrefetchScalarGridSpec(
        num_scalar_prefetch=1,
        grid=(bsz, n_groups),
        in_specs=[per_seq, per_seq, per_seq,
                  pl.BlockSpec((1, N_FOX_HEADS, 1), lambda b, g, pt: (b, 0, 0)),
                  pl.BlockSpec(tw.shape, lambda b, g, pt: (0, 0), pipeline_mode=pl.Buffered(1)),
                  hbm, hbm, hbm],
        out_specs=per_seq,
        scratch_shapes=[pltpu.VMEM((2, n, N_FOX_HEADS, PAGE_SIZE), F32),
                        pltpu.VMEM((2, n, FOX_WIDTH, PAGE_SIZE), F32),
                        pltpu.VMEM((2, n, N_FOX_HEADS, HEAD_DIM, PAGE_SIZE), F32),
                        pltpu.SemaphoreType.DMA((2, 3, n))]
                       + [pltpu.VMEM((N_FOX_HEADS, LANES), F32)] * 4
                       + [pltpu.VMEM((N_FOX_HEADS, HEAD_DIM, PAGE_SIZE), F32)],
    )
    return pl.pallas_call(
        _decode_attn_kernel,
        grid_spec=grid_spec,
        out_shape=jax.ShapeDtypeStruct((bsz, 1, FOX_WIDTH), F32),
        compiler_params=_params(2),
        name="decode_attn",
    )(page_table, q, knew, vnew, lfnew, tw, lf_cache, k_cache, v_cache)


def _mem_decode_kernel(q_ref, mkt_ref, mvt_ref, o_ref):
    qb = _head_rows(q_ref[0], SUBLANES, MEM_WIDTH).astype(BF16)
    s = jnp.dot(qb, mkt_ref[0].astype(BF16), preferred_element_type=F32)
    p = jnp.exp(s - jnp.max(s, axis=-1, keepdims=True))
    l = jnp.sum(p, axis=-1, keepdims=True)
    o = lax.dot_general(p.astype(BF16), mvt_ref[0].astype(BF16), _NT, preferred_element_type=F32) / l
    o = jnp.where(_head_mask(SUBLANES, MEM_WIDTH), o, 0.0)
    o_ref[0] = jnp.sum(o, axis=0, keepdims=True).astype(o_ref.dtype)


def _mem_decode(qm, mkt, mvt):
    bsz = qm.shape[0]
    return pl.pallas_call(
        _mem_decode_kernel,
        grid=(bsz,),
        in_specs=[pl.BlockSpec((1, 1, MEM_WIDTH), lambda b: (b, 0, 0)),
                  pl.BlockSpec((1, MEM_WIDTH, N_MEM), lambda b: (b, 0, 0)),
                  pl.BlockSpec((1, MEM_WIDTH, N_MEM), lambda b: (b, 0, 0))],
        out_specs=pl.BlockSpec((1, 1, MEM_WIDTH), lambda b: (b, 0, 0)),
        out_shape=jax.ShapeDtypeStruct((bsz, 1, MEM_WIDTH), BF16),
        compiler_params=_params(1),
        name="mem_decode",
    )(qm, mkt, mvt)


def _conv_norm_act(conv, gcn_ref, bcn_ref):
    mu = jnp.mean(conv, axis=-1, keepdims=True)
    d = conv - mu
    var = jnp.mean(d * d, axis=-1, keepdims=True)
    y = d * lax.rsqrt(var + EPS) * gcn_ref[...] + bcn_ref[...]
    return y * _sigmoid(y)


def _merge_project(x, yfox, yb, yc, g_ref, wa_ref, wb_ref, wc_ref, wo_ref, gffn_ref, x1_ref, h2_ref):
    ga = g_ref[:, :D_MODEL].astype(F32)
    gb = g_ref[:, D_MODEL:2 * D_MODEL].astype(F32)
    gc = g_ref[:, 2 * D_MODEL:].astype(F32)
    merged = (ga * jnp.dot(yfox, wa_ref[...], preferred_element_type=F32)
              + gb * jnp.dot(yb.astype(BF16), wb_ref[...], preferred_element_type=F32)
              + gc * jnp.dot(yc, wc_ref[...], preferred_element_type=F32))
    x1 = x + jnp.dot(merged.astype(BF16), wo_ref[...], preferred_element_type=F32)
    x1_ref[...] = x1
    h2_ref[...] = (_rms_scale(x1) * gffn_ref[...]).astype(BF16)


def _tail_prompt_kernel(x_ref, yfox_ref, u_ref, uhalo_ref, qm_ref, g_ref, mkt_ref, mv_ref,
                        wdw_ref, gcn_ref, bcn_ref, wa_ref, wb_ref, wc_ref, wo_ref, gffn_ref,
                        x1_ref, h2_ref, ubuf_ref, *, tiles_per_seq):
    i = pl.program_id(0)
    tm = x_ref.shape[0]
    halo = uhalo_ref[...]
    ubuf_ref[0:CONV_HALO, :] = jnp.where(i % tiles_per_seq == 0, jnp.zeros_like(halo), halo)
    ubuf_ref[CONV_HALO:, :] = u_ref[...]
    base = CONV_HALO - (CONV_WIDTH - 1)
    chunk = 64
    convs = []
    for r0 in range(0, tm, chunk):
        acc = jnp.zeros((chunk, CONV_CH), F32)
        for j in range(CONV_WIDTH):
            acc = acc + wdw_ref[j:j + 1, :] * ubuf_ref[base + r0 + j:base + r0 + j + chunk, :]
        convs.append(acc)
    yb = _conv_norm_act(jnp.concatenate(convs, axis=0), gcn_ref, bcn_ref)

    qm = qm_ref[...]
    lane = lax.broadcasted_iota(jnp.int32, (tm, MEM_WIDTH), 1)
    mkt = mkt_ref[0]
    mv = mv_ref[...]
    yc = jnp.zeros((tm, MEM_WIDTH), F32)
    for h in range(N_MEM_HEADS):
        in_head = (lane >= h * HEAD_DIM) & (lane < (h + 1) * HEAD_DIM)
        qh = jnp.where(in_head, qm, jnp.zeros((), qm.dtype))
        s = jnp.dot(qh, mkt, preferred_element_type=F32)
        p = jnp.exp(s - jnp.max(s, axis=-1, keepdims=True))
        o = jnp.dot(p.astype(BF16), mv, preferred_element_type=F32) / jnp.sum(p, axis=-1, keepdims=True)
        yc = jnp.where(in_head, o, yc)

    _merge_project(x_ref[...], yfox_ref[...], yb, yc.astype(BF16), g_ref,
                   wa_ref, wb_ref, wc_ref, wo_ref, gffn_ref, x1_ref, h2_ref)


def _tail_sample_kernel(x_ref, yfox_ref, u_ref, state_ref, yc_ref, g_ref,
                        wdw_ref, gcn_ref, bcn_ref, wa_ref, wb_ref, wc_ref, wo_ref, gffn_ref,
                        x1_ref, h2_ref):
    conv = wdw_ref[CONV_WIDTH - 1:CONV_WIDTH, :] * u_ref[...]
    for j in range(CONV_WIDTH - 1):
        conv = conv + wdw_ref[j:j + 1, :] * state_ref[j]
    yb = _conv_norm_act(conv, gcn_ref, bcn_ref)
    _merge_project(x_ref[...], yfox_ref[...], yb, yc_ref[...], g_ref,
                   wa_ref, wb_ref, wc_ref, wo_ref, gffn_ref, x1_ref, h2_ref)


def _tail_prompt(x2d, yfox, u, qm, g, mktb, mvb, tiles_per_seq, tm, *ws):
    n = x2d.shape[0]
    rows = lambda w: pl.BlockSpec((tm, w), lambda i: (i, 0))
    halo_blocks = tm // CONV_HALO
    return pl.pallas_call(
        functools.partial(_tail_prompt_kernel, tiles_per_seq=tiles_per_seq),
        grid=(n // tm,),
        in_specs=[rows(D_MODEL), rows(FOX_WIDTH), rows(CONV_CH),
                  pl.BlockSpec((CONV_HALO, CONV_CH), lambda i: (jnp.maximum(i * halo_blocks - 1, 0), 0)),
                  rows(MEM_WIDTH), rows(3 * D_MODEL),
                  pl.BlockSpec((1, MEM_WIDTH, N_MEM), lambda i: (i // tiles_per_seq, 0, 0)),
                  pl.BlockSpec((N_MEM, MEM_WIDTH), lambda i: (i // tiles_per_seq, 0))]
                 + [_resident(w.shape) for w in ws],
        out_specs=[rows(D_MODEL), rows(D_MODEL)],
        out_shape=[jax.ShapeDtypeStruct((n, D_MODEL), F32), jax.ShapeDtypeStruct((n, D_MODEL), BF16)],
        scratch_shapes=[pltpu.VMEM((CONV_HALO + tm, CONV_CH), F32)],
        compiler_params=_params(1),
        name="tail_prompt",
    )(x2d, yfox, u, u, qm, g, mktb, mvb, *ws)


def _tail_sample(x2d, yfox, u, state_t, yc, g, *ws):
    n = x2d.shape[0]
    full = lambda a: pl.BlockSpec(a.shape, lambda i: (0,) * a.ndim)
    ins = (x2d, yfox, u, state_t, yc, g)
    return pl.pallas_call(
        _tail_sample_kernel,
        grid=(1,),
        in_specs=[full(a) for a in ins] + [_resident(w.shape) for w in ws],
        out_specs=[pl.BlockSpec((n, D_MODEL), lambda i: (0, 0))] * 2,
        out_shape=[jax.ShapeDtypeStruct((n, D_MODEL), F32), jax.ShapeDtypeStruct((n, D_MODEL), BF16)],
        compiler_params=_params(1),
        name="tail_sample",
    )(*ins, *ws)


def _ffn_finish(x1_ref, acc_ref, gfin_ref, y_ref):
    x2 = x1_ref[...] + acc_ref[...]
    y_ref[...] = _rms_scale(x2) * gfin_ref[...]


def _ffn_prompt_kernel(h2_ref, halo_ref, x1_ref, wg_ref, wu_ref, wd_ref, wcv_ref, gfin_ref,
                       y_ref, gtail_ref, hcat_ref, gbuf_ref, acc_ref, *, tiles_per_seq):
    i = pl.program_id(0)
    tm = h2_ref.shape[0]
    halo = halo_ref[...]
    hcat_ref[0:FFN_HALO, :] = jnp.where(i % tiles_per_seq == 0, jnp.zeros_like(halo), halo)
    hcat_ref[FFN_HALO:, :] = h2_ref[...]
    acc_ref[...] = jnp.zeros_like(acc_ref)

    def chunk(c, carry):
        gbuf_ref[...] = jnp.dot(hcat_ref[...], wg_ref[c], preferred_element_type=F32)
        up = jnp.dot(h2_ref[...], wu_ref[c], preferred_element_type=F32)
        w = wcv_ref[c]
        cv = (w[0:1, :] * gbuf_ref[FFN_HALO - 2:FFN_HALO - 2 + tm, :]
              + w[1:2, :] * gbuf_ref[FFN_HALO - 1:FFN_HALO - 1 + tm, :]
              + w[2:3, :] * gbuf_ref[FFN_HALO:FFN_HALO + tm, :])
        act = (cv * _sigmoid(cv) * up).astype(BF16)
        acc_ref[...] += jnp.dot(act, wd_ref[c], preferred_element_type=F32)
        gtail_ref[0, c] = gbuf_ref[FFN_HALO + tm - SUBLANES:FFN_HALO + tm, :]
        return carry

    lax.fori_loop(0, wg_ref.shape[0], chunk, 0)
    _ffn_finish(x1_ref, acc_ref, gfin_ref, y_ref)


def _ffn_sample_kernel(h2_ref, state_ref, x1_ref, wg_ref, wu_ref, wd_ref, wcv_ref, gfin_ref,
                       y_ref, gt_ref, acc_ref):
    acc_ref[...] = jnp.zeros_like(acc_ref)

    def chunk(c, carry):
        h2 = h2_ref[...]
        gt = jnp.dot(h2, wg_ref[c], preferred_element_type=F32)
        up = jnp.dot(h2, wu_ref[c], preferred_element_type=F32)
        w = wcv_ref[c]
        cv = w[0:1, :] * state_ref[c, 0] + w[1:2, :] * state_ref[c, 1] + w[2:3, :] * gt
        act = (cv * _sigmoid(cv) * up).astype(BF16)
        acc_ref[...] += jnp.dot(act, wd_ref[c], preferred_element_type=F32)
        gt_ref[c] = gt
        return carry

    lax.fori_loop(0, wg_ref.shape[0], chunk, 0)
    _ffn_finish(x1_ref, acc_ref, gfin_ref, y_ref)


def _ffn_prompt(h2, x1, tiles_per_seq, tm, wg, wu, wd, wcv, gfin):
    n = h2.shape[0]
    nt = n // tm
    n_chunks = wg.shape[0]
    rows = lambda w: pl.BlockSpec((tm, w), lambda i: (i, 0))
    halo_blocks = tm // FFN_HALO
    return pl.pallas_call(
        functools.partial(_ffn_prompt_kernel, tiles_per_seq=tiles_per_seq),
        grid=(nt,),
        in_specs=[rows(D_MODEL),
                  pl.BlockSpec((FFN_HALO, D_MODEL), lambda i: (jnp.maximum(i * halo_blocks - 1, 0), 0)),
                  rows(D_MODEL), _resident(wg.shape), _resident(wu.shape), _resident(wd.shape),
                  _resident(wcv.shape), _resident(gfin.shape)],
        out_specs=[rows(D_MODEL), pl.BlockSpec((1, n_chunks, SUBLANES, FF_CHUNK), lambda i: (i, 0, 0, 0))],
        out_shape=[jax.ShapeDtypeStruct((n, D_MODEL), F32),
                   jax.ShapeDtypeStruct((nt, n_chunks, SUBLANES, FF_CHUNK), F32)],
        scratch_shapes=[pltpu.VMEM((FFN_HALO + tm, D_MODEL), BF16),
                        pltpu.VMEM((FFN_HALO + tm, FF_CHUNK), F32),
                        pltpu.VMEM((tm, D_MODEL), F32)],
        compiler_params=_params(1),
        name="ffn_prompt",
    )(h2, h2, x1, wg, wu, wd, wcv, gfin)


def _ffn_sample(h2, state_c, x1, wg, wu, wd, wcv, gfin):
    n = h2.shape[0]
    n_chunks = wg.shape[0]
    full = lambda a: pl.BlockSpec(a.shape, lambda i: (0,) * a.ndim)
    return pl.pallas_call(
        _ffn_sample_kernel,
        grid=(1,),
        in_specs=[full(h2), full(state_c), full(x1), _resident(wg.shape), _resident(wu.shape),
                  _resident(wd.shape), _resident(wcv.shape), _resident(gfin.shape)],
        out_specs=[pl.BlockSpec((n, D_MODEL), lambda i: (0, 0)),
                   pl.BlockSpec((n_chunks, n, FF_CHUNK), lambda i: (0, 0, 0))],
        out_shape=[jax.ShapeDtypeStruct((n, D_MODEL), F32),
                   jax.ShapeDtypeStruct((n_chunks, n, FF_CHUNK), F32)],
        scratch_shapes=[pltpu.VMEM((n, D_MODEL), F32)],
        compiler_params=_params(1),
        name="ffn_sample",
    )(h2, state_c, x1, wg, wu, wd, wcv, gfin)


def _pad_rows(a, rows):
    return jnp.concatenate([a, jnp.zeros((rows - a.shape[0],) + a.shape[1:], a.dtype)], axis=0)


def _heads_last(xt, batch, n_heads, n_pos):
    return xt.reshape(batch, n_heads, HEAD_DIM, n_pos).transpose(0, 3, 1, 2)


def kernel(x_prompt, x_sample, cache_fox_k, cache_fox_v, cache_fox_logf, state_conv, state_ffn_conv, cache_mem_k, cache_mem_v, page_table, mem_prompt, g_mix, w_in, b_f, w_conv_dw, g_conv_norm, b_conv_norm, g_mem, w_mem_kv, w_a, w_b, w_c, w_o, g_ffn, w_ffn_in, w_ffn_conv, w_down, g_final):
    depth = g_mix.shape[0]
    assert depth == 1, "kernel is written for the single-layer trunk the problem states"
    bp, seq, _ = x_prompt.shape
    bs, dec_seq, _ = x_sample.shape
    assert dec_seq == 1 and seq % FFN_ROW_TILE == 0 and seq % ATTN_BLOCK == 0 and bs % LANES == 0
    n_pool = cache_fox_k.shape[1]
    n_pages = page_table.shape[1]
    assert n_pages % PAGES_PER_STEP == 0 and cache_fox_k.shape[2] == PAGE_SIZE
    l = 0

    w = w_in[l]
    wt = w.T
    o_f = 3 * FOX_WIDTH
    o_glu = o_f + N_FOX_HEADS
    o_qm = o_glu + 2 * CONV_CH
    o_gate = o_qm + MEM_WIDTH
    front_w = (
        g_mix[l][None, :],
        w[:, :FOX_WIDTH].astype(BF16),
        wt[FOX_WIDTH:o_f].astype(BF16),
        w[:, 2 * FOX_WIDTH:o_f].astype(BF16),
        _pad_rows(wt[o_f:o_glu], BF16_SUBLANES).astype(BF16),
        b_f[l][:, None],
        w[:, o_glu:o_qm].astype(BF16),
        w[:, o_qm:o_gate].astype(BF16),
        w[:, o_gate:].astype(BF16),
    )
    tail_w = (_pad_rows(w_conv_dw[l], CONV_HALO), g_conv_norm[l][None, :], b_conv_norm[l][None, :],
              w_a[l].astype(BF16), w_b[l].astype(BF16), w_c[l].astype(BF16), w_o[l].astype(BF16),
              g_ffn[l][None, :])
    n_chunks = D_FF // FF_CHUNK
    wfi = w_ffn_in[l]
    wg = wfi[:, :D_FF].reshape(D_MODEL, n_chunks, FF_CHUNK).transpose(1, 0, 2).astype(BF16)
    wu = wfi[:, D_FF:].reshape(D_MODEL, n_chunks, FF_CHUNK).transpose(1, 0, 2).astype(BF16)
    wd = w_down[l].reshape(n_chunks, FF_CHUNK, D_MODEL).astype(BF16)
    wcv = _pad_rows(w_ffn_conv[l], SUBLANES).reshape(SUBLANES, n_chunks, FF_CHUNK).transpose(1, 0, 2)
    ffn_w = (wg, wu, wd, wcv, g_final[None, :])

    xp2 = x_prompt.reshape(bp * seq, D_MODEL)
    q_p, kt_p, ktb_p, vt_p, vb_p, lft_p, ct_p, u_p, qm_p, g_p = _front(xp2, bp, ROW_TILE, False, *front_w)
    wmt = w_mem_kv[l].T.astype(BF16)
    mkt_p, mvt_p, mktb_p, mvb_p = _memkv(mem_prompt.reshape(bp * N_MEM, D_MODEL), bp, g_mem[l][None, :],
                                         wmt, w_mem_kv[l][:, MEM_WIDTH:].astype(BF16))
    yfox_p = _fox_prompt(q_p, ktb_p, vb_p, ct_p.reshape(bp, FOX_WIDTH // LANES, 2, seq), bp, seq)
    x1_p, h2_p = _tail_prompt(xp2, yfox_p, u_p, qm_p, g_p, mktb_p, mvb_p, seq // ROW_TILE, ROW_TILE, *tail_w)
    ffn_tiles = seq // FFN_ROW_TILE
    y_p, gtail_p = _ffn_prompt(h2_p, x1_p, ffn_tiles, FFN_ROW_TILE, *ffn_w)

    xs2 = x_sample.reshape(bs, D_MODEL)
    (q_s, kt_s, _, vt_s, _, lft_s, _, u_s, qm_s, g_s, krow_s, vrow_s) = _front(xs2, 1, bs, True, *front_w)
    row3 = lambda a: a.reshape(bs, 1, a.shape[-1])
    yfox_s = _decode_attn(
        page_table, row3(q_s), row3(krow_s), row3(vrow_s), lft_s[0].T[:, :, None],
        cache_fox_logf[l].transpose(0, 2, 1),
        cache_fox_k[l].transpose(0, 2, 3, 1).reshape(n_pool, FOX_WIDTH, PAGE_SIZE),
        cache_fox_v[l].transpose(0, 2, 3, 1))
    yc_s = _mem_decode(row3(qm_s),
                       cache_mem_k[l].transpose(0, 2, 3, 1).reshape(bs, MEM_WIDTH, N_MEM),
                       cache_mem_v[l].transpose(0, 2, 3, 1).reshape(bs, MEM_WIDTH, N_MEM))
    x1_s, h2_s = _tail_sample(xs2, yfox_s.reshape(bs, FOX_WIDTH).astype(BF16), u_s,
                              state_conv[l].transpose(1, 0, 2), yc_s.reshape(bs, MEM_WIDTH), g_s, *tail_w)
    ffn_state_c = state_ffn_conv[l].reshape(bs, FFN_CONV_WIDTH - 1, n_chunks, FF_CHUNK).transpose(2, 1, 0, 3)
    y_s, gt_s = _ffn_sample(h2_s, ffn_state_c, x1_s, *ffn_w)

    y_prompt = y_p.reshape(bp, seq, D_MODEL)
    y_sample = y_s.reshape(bs, 1, D_MODEL)
    fox_k_p = _heads_last(kt_p, bp, N_FOX_HEADS, seq)[None]
    fox_v_p = _heads_last(vt_p, bp, N_FOX_HEADS, seq)[None]
    fox_logf_p = lft_p.transpose(0, 2, 1)[None]
    conv_state_p = u_p.reshape(bp, seq, CONV_CH)[:, seq - (CONV_WIDTH - 1):][None]
    gt_last = gtail_p.reshape(bp, ffn_tiles, n_chunks, SUBLANES, FF_CHUNK)[
        :, -1, :, SUBLANES - (FFN_CONV_WIDTH - 1):, :]
    ffn_state_p = gt_last.transpose(0, 2, 1, 3).reshape(1, bp, FFN_CONV_WIDTH - 1, D_FF)
    mem_k_p = _heads_last(mkt_p, bp, N_MEM_HEADS, N_MEM)[None]
    mem_v_p = _heads_last(mvt_p, bp, N_MEM_HEADS, N_MEM)[None]
    fox_k_s = _heads_last(kt_s, 1, N_FOX_HEADS, bs).reshape(1, bs, 1, N_FOX_HEADS, HEAD_DIM)
    fox_v_s = _heads_last(vt_s, 1, N_FOX_HEADS, bs).reshape(1, bs, 1, N_FOX_HEADS, HEAD_DIM)
    fox_logf_s = lft_s[0].T.reshape(1, bs, 1, N_FOX_HEADS)
    conv_state_s = jnp.concatenate([state_conv[l][:, 1:], u_s[:, None, :]], axis=1)[None]
    gt_new = gt_s.transpose(1, 0, 2).reshape(bs, 1, D_FF)
    ffn_state_s = jnp.concatenate([state_ffn_conv[l][:, 1:], gt_new], axis=1)[None]
    return (y_prompt, y_sample, fox_k_p, fox_v_p, fox_logf_p, conv_state_p, ffn_state_p, mem_k_p, mem_v_p,
            fox_k_s, fox_v_s, fox_logf_s, conv_state_s, ffn_state_s)
```

```python
import functools
from typing import Any, NamedTuple

import jax
import jax.numpy as jnp
from jax import lax
from jax.experimental import pallas as pl
from jax.experimental.pallas import tpu as pltpu

F32 = jnp.float32
BF16 = jnp.bfloat16

D_MODEL = 1024
HEAD_DIM = 64
N_FOX_HEADS = 8
FOX_WIDTH = N_FOX_HEADS * HEAD_DIM
CONV_CH = 256
CONV_WIDTH = 31
N_MEM = 256
N_MEM_HEADS = 4
MEM_WIDTH = N_MEM_HEADS * HEAD_DIM
D_FF = 3 * D_MODEL
FFN_CONV_WIDTH = 3
PAGE_SIZE = 128
EPS = 1e-6
ATTN_SCALE = HEAD_DIM ** -0.5
NEG_INF = -1e30

LANES = 128
SUBLANES = 8
BF16_SUBLANES = 16
ROW_TILE = 256
FFN_ROW_TILE = 512
ATTN_BLOCK = 512
CONV_HALO = 32
FFN_HALO = BF16_SUBLANES
FF_CHUNK = 512
PAGES_PER_STEP = 8
MEM_DECODE_SEQS = 8
VMEM_LIMIT_BYTES = 56 * 1024 * 1024

_NT = (((1,), (1,)), ((), ()))


def _params(n_axes):
    return pltpu.CompilerParams(dimension_semantics=("arbitrary",) * n_axes,
                                vmem_limit_bytes=VMEM_LIMIT_BYTES)


def _resident(shape):
    nd = len(shape)
    return pl.BlockSpec(shape, lambda *_: (0,) * nd, pipeline_mode=pl.Buffered(1))


def _rms_scale(x):
    return x * lax.rsqrt(jnp.mean(x * x, axis=-1, keepdims=True) + EPS)


def _sigmoid(x):
    return 1.0 / (1.0 + jnp.exp(-x))


def _log_sigmoid(x):
    return jnp.minimum(x, 0.0) - jnp.log(1.0 + jnp.exp(-jnp.abs(x)))


def _split3(x):
    hi = x.astype(BF16)
    r1 = x - hi.astype(F32)
    mid = r1.astype(BF16)
    lo = (r1 - mid.astype(F32)).astype(BF16)
    return hi, mid, lo


def _dot3(a, b01):
    return sum(jnp.dot(t, b01, preferred_element_type=F32) for t in _split3(a))


def _dot3_left(a01, b):
    return sum(jnp.dot(a01, t, preferred_element_type=F32) for t in _split3(b))


def _dot3_nt(a01, b):
    return sum(lax.dot_general(a01, t, _NT, preferred_element_type=F32) for t in _split3(b))


def _ones_where(cond):
    return jnp.where(cond, 1.0, 0.0).astype(BF16)


def _head_mask(n_rows, width):
    r = lax.broadcasted_iota(jnp.int32, (n_rows, width), 0)
    c = lax.broadcasted_iota(jnp.int32, (n_rows, width), 1)
    return (c >= r * HEAD_DIM) & (c < (r + 1) * HEAD_DIM)


def _head_rows(row, n_rows, width):
    return jnp.where(_head_mask(n_rows, width), jnp.broadcast_to(row.astype(F32), (n_rows, width)), 0.0)


def _heads_to_row(col, width):
    n_rows = col.shape[0]
    spread = jnp.where(_head_mask(n_rows, width), jnp.broadcast_to(col[:, 0:1], (n_rows, width)), 0.0)
    return jnp.sum(spread, axis=0, keepdims=True)


def _front_kernel(x_ref, gmix_ref, wq_ref, wkvt_ref, wv_ref, wft_ref, bf_ref, wglu_ref, wqm_ref, wgate_ref,
                  q_ref, kt_ref, ktb_ref, vt_ref, vb_ref, lft_ref, ct_ref, u_ref, qm_ref, g_ref, *rest,
                  tiles_per_seq, emit_rows):
    if emit_rows:
        krow_ref, vrow_ref, carry_ref = rest
    else:
        (carry_ref,) = rest
    i = pl.program_id(0)
    tm = x_ref.shape[0]
    h = (_rms_scale(x_ref[...]) * gmix_ref[...]).astype(BF16)

    q_ref[...] = (jnp.dot(h, wq_ref[...], preferred_element_type=F32) * ATTN_SCALE).astype(BF16)
    kvt = lax.dot_general(wkvt_ref[...], h, _NT, preferred_element_type=F32)
    kt = kvt[:FOX_WIDTH]
    kt_ref[0] = kt
    ktb_ref[0] = kt.astype(BF16)
    vt_ref[0] = kvt[FOX_WIDTH:]
    v = jnp.dot(h, wv_ref[...], preferred_element_type=F32)
    vb_ref[...] = v.astype(BF16)
    if emit_rows:
        vrow_ref[...] = v
        krow_ref[...] = lax.dot_general(h, wkvt_ref[0:FOX_WIDTH, :], _NT, preferred_element_type=F32)

    flt = lax.dot_general(wft_ref[...], h, _NT, preferred_element_type=F32)[:N_FOX_HEADS] + bf_ref[...]
    lft = _log_sigmoid(flt)
    lft_ref[0] = lft

    @pl.when(i % tiles_per_seq == 0)
    def _():
        carry_ref[...] = jnp.zeros_like(carry_ref)

    r = lax.broadcasted_iota(jnp.int32, (tm, tm), 0)
    c = lax.broadcasted_iota(jnp.int32, (tm, tm), 1)
    csum = _dot3(lft, _ones_where(r <= c)) + carry_ref[:, 0:1]
    ct_ref[0] = csum
    carry_ref[...] = jnp.broadcast_to(csum[:, tm - 1:tm], carry_ref.shape)

    glu = jnp.dot(h, wglu_ref[...], preferred_element_type=F32)
    u_ref[...] = glu[:, :CONV_CH] * _sigmoid(glu[:, CONV_CH:])
    qm = jnp.dot(h, wqm_ref[...], preferred_element_type=F32)
    qm_ref[...] = (qm * ATTN_SCALE).astype(BF16)
    g_ref[...] = _sigmoid(jnp.dot(h, wgate_ref[...], preferred_element_type=F32)).astype(BF16)


def _front(x2d, batch, tm, emit_rows, *weights):
    n = x2d.shape[0]
    seq = n // batch
    tiles_per_seq = seq // tm
    rows = lambda w: pl.BlockSpec((tm, w), lambda i: (i, 0))
    cols = lambda r: pl.BlockSpec((1, r, tm), lambda i: (i // tiles_per_seq, 0, i % tiles_per_seq))
    out = [
        (jax.ShapeDtypeStruct((n, FOX_WIDTH), BF16), rows(FOX_WIDTH)),
        (jax.ShapeDtypeStruct((batch, FOX_WIDTH, seq), F32), cols(FOX_WIDTH)),
        (jax.ShapeDtypeStruct((batch, FOX_WIDTH, seq), BF16), cols(FOX_WIDTH)),
        (jax.ShapeDtypeStruct((batch, FOX_WIDTH, seq), F32), cols(FOX_WIDTH)),
        (jax.ShapeDtypeStruct((n, FOX_WIDTH), BF16), rows(FOX_WIDTH)),
        (jax.ShapeDtypeStruct((batch, N_FOX_HEADS, seq), F32), cols(N_FOX_HEADS)),
        (jax.ShapeDtypeStruct((batch, N_FOX_HEADS, seq), F32), cols(N_FOX_HEADS)),
        (jax.ShapeDtypeStruct((n, CONV_CH), F32), rows(CONV_CH)),
        (jax.ShapeDtypeStruct((n, MEM_WIDTH), BF16), rows(MEM_WIDTH)),
        (jax.ShapeDtypeStruct((n, 3 * D_MODEL), BF16), rows(3 * D_MODEL)),
    ]
    if emit_rows:
        out += [(jax.ShapeDtypeStruct((n, FOX_WIDTH), F32), rows(FOX_WIDTH))] * 2
    return pl.pallas_call(
        functools.partial(_front_kernel, tiles_per_seq=tiles_per_seq, emit_rows=emit_rows),
        grid=(n // tm,),
        in_specs=[rows(D_MODEL)] + [_resident(w.shape) for w in weights],
        out_specs=[s for _, s in out],
        out_shape=[s for s, _ in out],
        scratch_shapes=[pltpu.VMEM((N_FOX_HEADS, LANES), F32)],
        compiler_params=_params(1),
        name="front_rows" if emit_rows else "front",
    )(x2d, *weights)


def _memkv_kernel(m_ref, g_ref, wt_ref, wv_ref, mkt_ref, mvt_ref, mktb_ref, mvb_ref):
    h = (_rms_scale(m_ref[...]) * g_ref[...]).astype(BF16)
    kvt = lax.dot_general(wt_ref[...], h, _NT, preferred_element_type=F32)
    mkt_ref[0] = kvt[:MEM_WIDTH]
    mvt_ref[0] = kvt[MEM_WIDTH:]
    mktb_ref[0] = kvt[:MEM_WIDTH].astype(BF16)
    mvb_ref[...] = jnp.dot(h, wv_ref[...], preferred_element_type=F32).astype(BF16)


def _memkv(mem2d, batch, gmem, wkvt, wv):
    rows = lambda w: pl.BlockSpec((N_MEM, w), lambda i: (i, 0))
    per_b = pl.BlockSpec((1, MEM_WIDTH, N_MEM), lambda i: (i, 0, 0))
    t_shape = (batch, MEM_WIDTH, N_MEM)
    return pl.pallas_call(
        _memkv_kernel,
        grid=(batch,),
        in_specs=[rows(D_MODEL), _resident(gmem.shape), _resident(wkvt.shape), _resident(wv.shape)],
        out_specs=[per_b, per_b, per_b, rows(MEM_WIDTH)],
        out_shape=[jax.ShapeDtypeStruct(t_shape, F32), jax.ShapeDtypeStruct(t_shape, F32),
                   jax.ShapeDtypeStruct(t_shape, BF16), jax.ShapeDtypeStruct((batch * N_MEM, MEM_WIDTH), BF16)],
        compiler_params=_params(1),
        name="memkv",
    )(mem2d, gmem, wkvt, wv)


def _fox_prompt_kernel(pt_ref, q_ref, kt_ref, v_ref, ct_ref, *refs, first, last):
    o_ref = refs[N_DECODE_INPUTS]
    dec = _decode_refs(pt_ref, refs[:N_DECODE_INPUTS], refs[N_DECODE_INPUTS + 1], refs[N_DECODE_INPUTS + 2:])
    nq = pl.num_programs(2)
    i = pl.program_id(2)
    steps_before = (pl.program_id(0) * pl.num_programs(1) + pl.program_id(1)) * (nq * (nq + 1) // 2) \
        + i * (i + 1) // 2
    tq = q_ref.shape[0]
    tk = tq
    q = q_ref[...]
    lane = lax.broadcasted_iota(jnp.int32, (tq, LANES), 1)
    zero = jnp.zeros((), q.dtype)
    q2 = jnp.concatenate([jnp.where(lane < HEAD_DIM, q, zero), jnp.where(lane >= HEAD_DIM, q, zero)], axis=0)

    def step(j, carry, masked):
        m, l, acc = carry
        group = first + steps_before + j
        _decode_fetch(dec, group, first, last)
        _decode_compute(dec, group, first)
        start = pl.multiple_of(j * tk, tk)
        s = jnp.dot(q2, kt_ref[0, :, pl.ds(start, tk)], preferred_element_type=F32)
        ck = ct_ref[0, 0, :, pl.ds(start, tk)]
        z = jnp.concatenate([s[:tq] - ck[0:1], s[tq:] - ck[1:2]], axis=0)
        if masked:
            row = lax.broadcasted_iota(jnp.int32, (2 * tq, tk), 0)
            col = lax.broadcasted_iota(jnp.int32, (2 * tq, tk), 1)
            z = jnp.where(col <= (row & (tq - 1)), z, NEG_INF)
        m_new = jnp.maximum(m, jnp.max(z, axis=-1, keepdims=True))
        alpha = jnp.exp(m - m_new)
        p = jnp.exp(z - m_new)
        l = alpha * l + jnp.sum(p, axis=-1, keepdims=True)
        acc = alpha * acc + jnp.dot(p.astype(BF16), v_ref[pl.ds(start, tk), :], preferred_element_type=F32)
        _decode_finish(dec, group)
        return m_new, l, acc

    init = (jnp.full((2 * tq, 1), NEG_INF, F32), jnp.zeros((2 * tq, 1), F32), jnp.zeros((2 * tq, LANES), F32))
    carry = lax.fori_loop(0, i, functools.partial(step, masked=False), init)
    _, l, acc = step(i, carry, True)
    o = acc / l
    o_ref[...] = jnp.where(lane < HEAD_DIM, o[:tq], o[tq:]).astype(o_ref.dtype)


def _fox_prompt_steps(batch, seq):
    nq = seq // ATTN_BLOCK
    return batch * (FOX_WIDTH // LANES) * (nq * (nq + 1) // 2)


def _fox_prompt(q, ktb, vb, ct, batch, seq, page_table, decode_in, first):
    assert ATTN_BLOCK & (ATTN_BLOCK - 1) == 0
    nq = seq // ATTN_BLOCK
    n_pairs = FOX_WIDTH // LANES
    last = first + _fox_prompt_steps(batch, seq) - 1
    d_args, d_specs, d_out_spec, d_out_shape, d_scratch = _decode_operands(page_table, *decode_in)
    tile = pl.BlockSpec((ATTN_BLOCK, LANES), lambda b, p, i, pt: (b * nq + i, p))
    grid_spec = pltpu.PrefetchScalarGridSpec(
        num_scalar_prefetch=1,
        grid=(batch, n_pairs, nq),
        in_specs=[
            tile,
            pl.BlockSpec((1, LANES, seq), lambda b, p, i, pt: (b, p, 0)),
            pl.BlockSpec((seq, LANES), lambda b, p, i, pt: (b, p)),
            pl.BlockSpec((1, 1, 2, seq), lambda b, p, i, pt: (b, p, 0, 0)),
        ] + d_specs,
        out_specs=[tile, d_out_spec],
        scratch_shapes=d_scratch,
    )
    return pl.pallas_call(
        functools.partial(_fox_prompt_kernel, first=first, last=last),
        grid_spec=grid_spec,
        out_shape=[jax.ShapeDtypeStruct((batch * seq, FOX_WIDTH), BF16), d_out_shape],
        input_output_aliases={4 + len(d_args): 1},
        compiler_params=_params(3),
        name="fox_prompt",
    )(page_table, q, ktb, vb, ct, *d_args)


class _Decode(NamedTuple):
    pt: Any
    q: Any
    knew: Any
    vnew: Any
    lfnew: Any
    tw: Any
    lf_hbm: Any
    k_hbm: Any
    v_hbm: Any
    o_hbm: Any
    lfbuf: Any
    kbuf: Any
    vbuf: Any
    sem: Any
    orow: Any
    osem: Any
    m: Any
    l: Any
    cn: Any
    carry: Any
    acc: Any


N_DECODE_INPUTS = 9
N_DECODE_SCRATCH = 11


def _decode_refs(pt_ref, ins, o_hbm, scratch):
    assert len(ins) == N_DECODE_INPUTS and len(scratch) == N_DECODE_SCRATCH
    return _Decode(pt_ref, *ins[:N_DECODE_INPUTS - 1], o_hbm, *scratch)


def _page_copies(r, group, slot):
    n = PAGES_PER_STEP
    n_groups = r.pt.shape[1] // n
    if group is not None:
        seq_i = group // n_groups
        first = (n_groups - 1 - group % n_groups) * n
    copies = []
    for j in range(n):
        page = 0 if group is None else r.pt[seq_i, first + j]
        copies.append(pltpu.make_async_copy(r.lf_hbm.at[page], r.lfbuf.at[slot, j], r.sem.at[slot, 0, j]))
        copies.append(pltpu.make_async_copy(r.k_hbm.at[page], r.kbuf.at[slot, j], r.sem.at[slot, 1, j]))
        copies.append(pltpu.make_async_copy(r.v_hbm.at[page], r.vbuf.at[slot, j], r.sem.at[slot, 2, j]))
    return copies


def _start_all(copies):
    for idx, cp in enumerate(copies):
        cp.start(priority=idx % 2)


def _decode_fetch(r, group, first, last):
    slot = (group - first) % 2

    @pl.when(group == first)
    def _():
        _start_all(_page_copies(r, first, 0))

    @pl.when(group < last)
    def _():
        _start_all(_page_copies(r, group + 1, 1 - slot))

    for cp in _page_copies(r, None, slot):
        cp.wait()

    n_groups = r.pt.shape[1] // PAGES_PER_STEP
    seq_i = group // n_groups

    @pl.when(group % n_groups == 0)
    def _():
        qf = _head_rows(r.q[seq_i], N_FOX_HEADS, FOX_WIDTH)
        s_new = jnp.sum(qf * r.knew[seq_i], axis=-1, keepdims=True)
        r.m[...] = jnp.broadcast_to(s_new, r.m.shape)
        r.l[...] = jnp.ones_like(r.l)
        r.cn[...] = jnp.ones_like(r.cn)
        r.carry[...] = jnp.broadcast_to(r.lfnew[seq_i], r.carry.shape)
        r.acc[...] = jnp.zeros_like(r.acc)


def _decode_compute(r, group, first):
    n = PAGES_PER_STEP
    n_groups = r.pt.shape[1] // n
    slot = (group - first) % 2
    qb = _head_rows(r.q[group // n_groups], N_FOX_HEADS, FOX_WIDTH).astype(BF16)
    x = r.lfbuf[slot].reshape(n * N_FOX_HEADS, PAGE_SIZE)
    wt = _dot3(x, r.tw[...])
    later = r.carry[...]
    scores = [None] * n
    for j in reversed(range(n)):
        rows = slice(j * N_FOX_HEADS, (j + 1) * N_FOX_HEADS)
        bias = wt[rows, :PAGE_SIZE] + later
        later = later + wt[rows, PAGE_SIZE:]
        scores[j] = jnp.dot(qb, r.kbuf[slot, j].astype(BF16), preferred_element_type=F32) + bias
    r.carry[...] = later
    z = jnp.concatenate(scores, axis=-1)
    m_old = r.m[...]
    m_new = jnp.maximum(m_old, jnp.max(z, axis=-1, keepdims=True))
    alpha = jnp.exp(m_old - m_new)
    p = jnp.exp(z - m_new[:, 0:1])
    r.l[...] = alpha * r.l[...] + jnp.sum(p, axis=-1, keepdims=True)
    r.cn[...] = alpha * r.cn[...]
    r.m[...] = m_new
    for h in range(N_FOX_HEADS):
        a = r.acc[h] * alpha[h:h + 1, :]
        for j in range(n):
            a = a + r.vbuf[slot, j, h] * p[h:h + 1, j * PAGE_SIZE:(j + 1) * PAGE_SIZE]
        r.acc[h] = a


def _decode_finish(r, group):
    n_groups = r.pt.shape[1] // PAGES_PER_STEP
    seq_i = group // n_groups

    @pl.when(group % n_groups == n_groups - 1)
    def _():
        acc2d = r.acc[...].reshape(FOX_WIDTH, PAGE_SIZE)
        ones = jnp.ones((BF16_SUBLANES, PAGE_SIZE), BF16)
        past = _dot3_nt(ones, acc2d)[0:1]
        new = _heads_to_row(r.cn[...], FOX_WIDTH) * r.vnew[seq_i]
        r.orow[...] = (past + new) / _heads_to_row(r.l[...], FOX_WIDTH)
        cp = pltpu.make_async_copy(r.orow, r.o_hbm.at[seq_i], r.osem)
        cp.start()
        cp.wait()


def _decode_operands(page_table, q, knew, vnew, lfnew, lf_cache, k_cache, v_cache, o_prev):
    del page_table
    n = PAGES_PER_STEP
    pos = jnp.arange(PAGE_SIZE)
    tw = jnp.concatenate([_ones_where(pos[:, None] > pos[None, :]),
                          jnp.ones((PAGE_SIZE, PAGE_SIZE), BF16)], axis=1)
    hbm = pl.BlockSpec(memory_space=pl.ANY)
    small = (q, knew, vnew, lfnew, tw)
    args = small + (lf_cache, k_cache, v_cache, o_prev)
    in_specs = [_resident(a.shape) for a in small] + [hbm] * 4
    scratch = ([pltpu.VMEM((2, n, N_FOX_HEADS, PAGE_SIZE), F32),
                pltpu.VMEM((2, n, FOX_WIDTH, PAGE_SIZE), F32),
                pltpu.VMEM((2, n, N_FOX_HEADS, HEAD_DIM, PAGE_SIZE), F32),
                pltpu.SemaphoreType.DMA((2, 3, n)),
                pltpu.VMEM((1, FOX_WIDTH), F32),
                pltpu.SemaphoreType.DMA(())]
               + [pltpu.VMEM((N_FOX_HEADS, LANES), F32)] * 4
               + [pltpu.VMEM((N_FOX_HEADS, HEAD_DIM, PAGE_SIZE), F32)])
    assert len(args) == N_DECODE_INPUTS and len(scratch) == N_DECODE_SCRATCH
    return args, in_specs, hbm, jax.ShapeDtypeStruct(o_prev.shape, o_prev.dtype), scratch


def _decode_attn_kernel(pt_ref, *refs, first, last):
    r = _decode_refs(pt_ref, refs[:N_DECODE_INPUTS], refs[N_DECODE_INPUTS], refs[N_DECODE_INPUTS + 1:])
    group = first + pl.program_id(0)
    _decode_fetch(r, group, first, last)
    _decode_compute(r, group, first)
    _decode_finish(r, group)


def _decode_attn(page_table, decode_in, first, last):
    args, in_specs, out_spec, out_shape, scratch = _decode_operands(page_table, *decode_in)
    grid_spec = pltpu.PrefetchScalarGridSpec(
        num_scalar_prefetch=1, grid=(last - first + 1,), in_specs=in_specs, out_specs=out_spec,
        scratch_shapes=scratch)
    return pl.pallas_call(
        functools.partial(_decode_attn_kernel, first=first, last=last),
        grid_spec=grid_spec,
        out_shape=out_shape,
        input_output_aliases={len(args): 0},
        compiler_params=_params(1),
        name="decode_attn",
    )(page_table, *args)


def _mem_decode_kernel(q_ref, mkt_ref, mvt_ref, o_ref):
    for b in range(q_ref.shape[0]):
        qb = _head_rows(q_ref[b], SUBLANES, MEM_WIDTH).astype(BF16)
        s = jnp.dot(qb, mkt_ref[b].astype(BF16), preferred_element_type=F32)
        p = jnp.exp(s - jnp.max(s, axis=-1, keepdims=True))
        l = jnp.sum(p, axis=-1, keepdims=True)
        o = lax.dot_general(p.astype(BF16), mvt_ref[b].astype(BF16), _NT, preferred_element_type=F32) / l
        o = jnp.where(_head_mask(SUBLANES, MEM_WIDTH), o, 0.0)
        o_ref[b] = jnp.sum(o, axis=0, keepdims=True).astype(o_ref.dtype)


def _mem_decode(qm, mkt, mvt):
    bsz = qm.shape[0]
    per_step = MEM_DECODE_SEQS
    return pl.pallas_call(
        _mem_decode_kernel,
        grid=(bsz // per_step,),
        in_specs=[pl.BlockSpec((per_step, 1, MEM_WIDTH), lambda b: (b, 0, 0)),
                  pl.BlockSpec((per_step, MEM_WIDTH, N_MEM), lambda b: (b, 0, 0)),
                  pl.BlockSpec((per_step, MEM_WIDTH, N_MEM), lambda b: (b, 0, 0))],
        out_specs=pl.BlockSpec((per_step, 1, MEM_WIDTH), lambda b: (b, 0, 0)),
        out_shape=jax.ShapeDtypeStruct((bsz, 1, MEM_WIDTH), BF16),
        compiler_params=_params(1),
        name="mem_decode",
    )(qm, mkt, mvt)


def _conv_norm_act(conv, gcn_ref, bcn_ref):
    mu = jnp.mean(conv, axis=-1, keepdims=True)
    d = conv - mu
    var = jnp.mean(d * d, axis=-1, keepdims=True)
    y = d * lax.rsqrt(var + EPS) * gcn_ref[...] + bcn_ref[...]
    return y * _sigmoid(y)


def _merge_project(x, yfox, yb, yc, g_ref, wa_ref, wb_ref, wc_ref, wo_ref, gffn_ref, x1_ref, h2_ref):
    ga = g_ref[:, :D_MODEL].astype(F32)
    gb = g_ref[:, D_MODEL:2 * D_MODEL].astype(F32)
    gc = g_ref[:, 2 * D_MODEL:].astype(F32)
    merged = (ga * jnp.dot(yfox, wa_ref[...], preferred_element_type=F32)
              + gb * jnp.dot(yb.astype(BF16), wb_ref[...], preferred_element_type=F32)
              + gc * jnp.dot(yc, wc_ref[...], preferred_element_type=F32))
    x1 = x + jnp.dot(merged.astype(BF16), wo_ref[...], preferred_element_type=F32)
    x1_ref[...] = x1
    h2_ref[...] = (_rms_scale(x1) * gffn_ref[...]).astype(BF16)


def _tail_prompt_kernel(x_ref, yfox_ref, u_ref, uhalo_ref, qm_ref, g_ref, mkt_ref, mv_ref,
                        wdw_ref, gcn_ref, bcn_ref, wa_ref, wb_ref, wc_ref, wo_ref, gffn_ref,
                        x1_ref, h2_ref, ubuf_ref, *, tiles_per_seq):
    i = pl.program_id(0)
    tm = x_ref.shape[0]
    halo = uhalo_ref[...]
    ubuf_ref[0:CONV_HALO, :] = jnp.where(i % tiles_per_seq == 0, jnp.zeros_like(halo), halo)
    ubuf_ref[CONV_HALO:, :] = u_ref[...]
    base = CONV_HALO - (CONV_WIDTH - 1)
    chunk = 64
    convs = []
    for r0 in range(0, tm, chunk):
        acc = jnp.zeros((chunk, CONV_CH), F32)
        for j in range(CONV_WIDTH):
            acc = acc + wdw_ref[j:j + 1, :] * ubuf_ref[base + r0 + j:base + r0 + j + chunk, :]
        convs.append(acc)
    yb = _conv_norm_act(jnp.concatenate(convs, axis=0), gcn_ref, bcn_ref)

    qm = qm_ref[...]
    lane = lax.broadcasted_iota(jnp.int32, (tm, MEM_WIDTH), 1)
    mkt = mkt_ref[0]
    mv = mv_ref[...]
    yc = jnp.zeros((tm, MEM_WIDTH), F32)
    for h in range(N_MEM_HEADS):
        in_head = (lane >= h * HEAD_DIM) & (lane < (h + 1) * HEAD_DIM)
        qh = jnp.where(in_head, qm, jnp.zeros((), qm.dtype))
        s = jnp.dot(qh, mkt, preferred_element_type=F32)
        p = jnp.exp(s - jnp.max(s, axis=-1, keepdims=True))
        o = jnp.dot(p.astype(BF16), mv, preferred_element_type=F32) / jnp.sum(p, axis=-1, keepdims=True)
        yc = jnp.where(in_head, o, yc)

    _merge_project(x_ref[...], yfox_ref[...], yb, yc.astype(BF16), g_ref,
                   wa_ref, wb_ref, wc_ref, wo_ref, gffn_ref, x1_ref, h2_ref)


def _tail_sample_kernel(x_ref, yfox_ref, u_ref, state_ref, yc_ref, g_ref,
                        wdw_ref, gcn_ref, bcn_ref, wa_ref, wb_ref, wc_ref, wo_ref, gffn_ref,
                        x1_ref, h2_ref):
    conv = wdw_ref[CONV_WIDTH - 1:CONV_WIDTH, :] * u_ref[...]
    for j in range(CONV_WIDTH - 1):
        conv = conv + wdw_ref[j:j + 1, :] * state_ref[j]
    yb = _conv_norm_act(conv, gcn_ref, bcn_ref)
    _merge_project(x_ref[...], yfox_ref[...], yb, yc_ref[...], g_ref,
                   wa_ref, wb_ref, wc_ref, wo_ref, gffn_ref, x1_ref, h2_ref)


def _tail_prompt(x2d, yfox, u, qm, g, mktb, mvb, tiles_per_seq, tm, *ws):
    n = x2d.shape[0]
    rows = lambda w: pl.BlockSpec((tm, w), lambda i: (i, 0))
    halo_blocks = tm // CONV_HALO
    return pl.pallas_call(
        functools.partial(_tail_prompt_kernel, tiles_per_seq=tiles_per_seq),
        grid=(n // tm,),
        in_specs=[rows(D_MODEL), rows(FOX_WIDTH), rows(CONV_CH),
                  pl.BlockSpec((CONV_HALO, CONV_CH), lambda i: (jnp.maximum(i * halo_blocks - 1, 0), 0)),
                  rows(MEM_WIDTH), rows(3 * D_MODEL),
                  pl.BlockSpec((1, MEM_WIDTH, N_MEM), lambda i: (i // tiles_per_seq, 0, 0)),
                  pl.BlockSpec((N_MEM, MEM_WIDTH), lambda i: (i // tiles_per_seq, 0))]
                 + [_resident(w.shape) for w in ws],
        out_specs=[rows(D_MODEL), rows(D_MODEL)],
        out_shape=[jax.ShapeDtypeStruct((n, D_MODEL), F32), jax.ShapeDtypeStruct((n, D_MODEL), BF16)],
        scratch_shapes=[pltpu.VMEM((CONV_HALO + tm, CONV_CH), F32)],
        compiler_params=_params(1),
        name="tail_prompt",
    )(x2d, yfox, u, u, qm, g, mktb, mvb, *ws)


def _tail_sample(x2d, yfox, u, state_t, yc, g, *ws):
    n = x2d.shape[0]
    full = lambda a: pl.BlockSpec(a.shape, lambda i: (0,) * a.ndim)
    ins = (x2d, yfox, u, state_t, yc, g)
    return pl.pallas_call(
        _tail_sample_kernel,
        grid=(1,),
        in_specs=[full(a) for a in ins] + [_resident(w.shape) for w in ws],
        out_specs=[pl.BlockSpec((n, D_MODEL), lambda i: (0, 0))] * 2,
        out_shape=[jax.ShapeDtypeStruct((n, D_MODEL), F32), jax.ShapeDtypeStruct((n, D_MODEL), BF16)],
        compiler_params=_params(1),
        name="tail_sample",
    )(*ins, *ws)


def _ffn_finish(x1_ref, acc_ref, gfin_ref, y_ref):
    x2 = x1_ref[...] + acc_ref[...]
    y_ref[...] = _rms_scale(x2) * gfin_ref[...]


def _ffn_prompt_kernel(pt_ref, h2_ref, halo_ref, x1_ref, wg_ref, wu_ref, wd_ref, wcv_ref, gfin_ref, *refs,
                       tiles_per_seq, first, last):
    y_ref, gtail_ref, o_hbm, hcat_ref, gbuf_ref, acc_ref = refs[N_DECODE_INPUTS:N_DECODE_INPUTS + 6]
    dec = _decode_refs(pt_ref, refs[:N_DECODE_INPUTS], o_hbm, refs[N_DECODE_INPUTS + 6:])
    n_chunks = wg_ref.shape[0]
    i = pl.program_id(0)
    tm = h2_ref.shape[0]
    halo = halo_ref[...]
    hcat_ref[0:FFN_HALO, :] = jnp.where(i % tiles_per_seq == 0, jnp.zeros_like(halo), halo)
    hcat_ref[FFN_HALO:, :] = h2_ref[...]
    acc_ref[...] = jnp.zeros_like(acc_ref)

    def chunk(c, carry):
        group = first + i * n_chunks + c
        _decode_fetch(dec, group, first, last)
        _decode_compute(dec, group, first)
        gbuf_ref[...] = jnp.dot(hcat_ref[...], wg_ref[c], preferred_element_type=F32)
        up = jnp.dot(h2_ref[...], wu_ref[c], preferred_element_type=F32)
        w = wcv_ref[c]
        cv = (w[0:1, :] * gbuf_ref[FFN_HALO - 2:FFN_HALO - 2 + tm, :]
              + w[1:2, :] * gbuf_ref[FFN_HALO - 1:FFN_HALO - 1 + tm, :]
              + w[2:3, :] * gbuf_ref[FFN_HALO:FFN_HALO + tm, :])
        act = (cv * _sigmoid(cv) * up).astype(BF16)
        acc_ref[...] += jnp.dot(act, wd_ref[c], preferred_element_type=F32)
        gtail_ref[0, c] = gbuf_ref[FFN_HALO + tm - SUBLANES:FFN_HALO + tm, :]
        _decode_finish(dec, group)
        return carry

    lax.fori_loop(0, n_chunks, chunk, 0)
    _ffn_finish(x1_ref, acc_ref, gfin_ref, y_ref)


def _ffn_sample_kernel(h2_ref, state_ref, x1_ref, wg_ref, wu_ref, wd_ref, wcv_ref, gfin_ref,
                       y_ref, gt_ref, acc_ref):
    acc_ref[...] = jnp.zeros_like(acc_ref)

    def chunk(c, carry):
        h2 = h2_ref[...]
        gt = jnp.dot(h2, wg_ref[c], preferred_element_type=F32)
        up = jnp.dot(h2, wu_ref[c], preferred_element_type=F32)
        w = wcv_ref[c]
        cv = w[0:1, :] * state_ref[c, 0] + w[1:2, :] * state_ref[c, 1] + w[2:3, :] * gt
        act = (cv * _sigmoid(cv) * up).astype(BF16)
        acc_ref[...] += jnp.dot(act, wd_ref[c], preferred_element_type=F32)
        gt_ref[c] = gt
        return carry

    lax.fori_loop(0, wg_ref.shape[0], chunk, 0)
    _ffn_finish(x1_ref, acc_ref, gfin_ref, y_ref)


def _ffn_prompt_steps(n_rows, tm):
    return (n_rows // tm) * (D_FF // FF_CHUNK)


def _ffn_prompt(h2, x1, tiles_per_seq, tm, wg, wu, wd, wcv, gfin, page_table, decode_in, first):
    n = h2.shape[0]
    nt = n // tm
    n_chunks = wg.shape[0]
    last = first + _ffn_prompt_steps(n, tm) - 1
    d_args, d_specs, d_out_spec, d_out_shape, d_scratch = _decode_operands(page_table, *decode_in)
    rows = lambda w: pl.BlockSpec((tm, w), lambda i, pt: (i, 0))
    halo_blocks = tm // FFN_HALO
    grid_spec = pltpu.PrefetchScalarGridSpec(
        num_scalar_prefetch=1,
        grid=(nt,),
        in_specs=[rows(D_MODEL),
                  pl.BlockSpec((FFN_HALO, D_MODEL), lambda i, pt: (jnp.maximum(i * halo_blocks - 1, 0), 0)),
                  rows(D_MODEL), _resident(wg.shape), _resident(wu.shape), _resident(wd.shape),
                  _resident(wcv.shape), _resident(gfin.shape)] + d_specs,
        out_specs=[rows(D_MODEL), pl.BlockSpec((1, n_chunks, SUBLANES, FF_CHUNK), lambda i, pt: (i, 0, 0, 0)),
                   d_out_spec],
        scratch_shapes=[pltpu.VMEM((FFN_HALO + tm, D_MODEL), BF16),
                        pltpu.VMEM((FFN_HALO + tm, FF_CHUNK), F32),
                        pltpu.VMEM((tm, D_MODEL), F32)] + d_scratch,
    )
    return pl.pallas_call(
        functools.partial(_ffn_prompt_kernel, tiles_per_seq=tiles_per_seq, first=first, last=last),
        grid_spec=grid_spec,
        out_shape=[jax.ShapeDtypeStruct((n, D_MODEL), F32),
                   jax.ShapeDtypeStruct((nt, n_chunks, SUBLANES, FF_CHUNK), F32), d_out_shape],
        input_output_aliases={8 + len(d_args): 2},
        compiler_params=_params(1),
        name="ffn_prompt",
    )(page_table, h2, h2, x1, wg, wu, wd, wcv, gfin, *d_args)


def _ffn_sample(h2, state_c, x1, wg, wu, wd, wcv, gfin):
    n = h2.shape[0]
    n_chunks = wg.shape[0]
    full = lambda a: pl.BlockSpec(a.shape, lambda i: (0,) * a.ndim)
    return pl.pallas_call(
        _ffn_sample_kernel,
        grid=(1,),
        in_specs=[full(h2), full(state_c), full(x1), _resident(wg.shape), _resident(wu.shape),
                  _resident(wd.shape), _resident(wcv.shape), _resident(gfin.shape)],
        out_specs=[pl.BlockSpec((n, D_MODEL), lambda i: (0, 0)),
                   pl.BlockSpec((n_chunks, n, FF_CHUNK), lambda i: (0, 0, 0))],
        out_shape=[jax.ShapeDtypeStruct((n, D_MODEL), F32),
                   jax.ShapeDtypeStruct((n_chunks, n, FF_CHUNK), F32)],
        scratch_shapes=[pltpu.VMEM((n, D_MODEL), F32)],
        compiler_params=_params(1),
        name="ffn_sample",
    )(h2, state_c, x1, wg, wu, wd, wcv, gfin)


def _pad_rows(a, rows):
    return jnp.concatenate([a, jnp.zeros((rows - a.shape[0],) + a.shape[1:], a.dtype)], axis=0)


def _heads_last(xt, batch, n_heads, n_pos):
    return xt.reshape(batch, n_heads, HEAD_DIM, n_pos).transpose(0, 3, 1, 2)


def kernel(x_prompt, x_sample, cache_fox_k, cache_fox_v, cache_fox_logf, state_conv, state_ffn_conv, cache_mem_k, cache_mem_v, page_table, mem_prompt, g_mix, w_in, b_f, w_conv_dw, g_conv_norm, b_conv_norm, g_mem, w_mem_kv, w_a, w_b, w_c, w_o, g_ffn, w_ffn_in, w_ffn_conv, w_down, g_final):
    depth = g_mix.shape[0]
    assert depth == 1, "kernel is written for the single-layer trunk the problem states"
    bp, seq, _ = x_prompt.shape
    bs, dec_seq, _ = x_sample.shape
    assert dec_seq == 1 and seq % FFN_ROW_TILE == 0 and seq % ATTN_BLOCK == 0 and bs % LANES == 0
    n_pool = cache_fox_k.shape[1]
    n_pages = page_table.shape[1]
    assert n_pages % PAGES_PER_STEP == 0 and cache_fox_k.shape[2] == PAGE_SIZE
    l = 0

    w = w_in[l]
    wt = w.T
    o_f = 3 * FOX_WIDTH
    o_glu = o_f + N_FOX_HEADS
    o_qm = o_glu + 2 * CONV_CH
    o_gate = o_qm + MEM_WIDTH
    front_w = (
        g_mix[l][None, :],
        w[:, :FOX_WIDTH].astype(BF16),
        wt[FOX_WIDTH:o_f].astype(BF16),
        w[:, 2 * FOX_WIDTH:o_f].astype(BF16),
        _pad_rows(wt[o_f:o_glu], BF16_SUBLANES).astype(BF16),
        b_f[l][:, None],
        w[:, o_glu:o_qm].astype(BF16),
        w[:, o_qm:o_gate].astype(BF16),
        w[:, o_gate:].astype(BF16),
    )
    tail_w = (_pad_rows(w_conv_dw[l], CONV_HALO), g_conv_norm[l][None, :], b_conv_norm[l][None, :],
              w_a[l].astype(BF16), w_b[l].astype(BF16), w_c[l].astype(BF16), w_o[l].astype(BF16),
              g_ffn[l][None, :])
    n_chunks = D_FF // FF_CHUNK
    wfi = w_ffn_in[l]
    wg = wfi[:, :D_FF].reshape(D_MODEL, n_chunks, FF_CHUNK).transpose(1, 0, 2).astype(BF16)
    wu = wfi[:, D_FF:].reshape(D_MODEL, n_chunks, FF_CHUNK).transpose(1, 0, 2).astype(BF16)
    wd = w_down[l].reshape(n_chunks, FF_CHUNK, D_MODEL).astype(BF16)
    wcv = _pad_rows(w_ffn_conv[l], SUBLANES).reshape(SUBLANES, n_chunks, FF_CHUNK).transpose(1, 0, 2)
    ffn_w = (wg, wu, wd, wcv, g_final[None, :])

    xs2 = x_sample.reshape(bs, D_MODEL)
    (q_s, kt_s, _, vt_s, _, lft_s, _, u_s, qm_s, g_s, krow_s, vrow_s) = _front(xs2, 1, bs, True, *front_w)
    row3 = lambda a: a.reshape(bs, 1, a.shape[-1])
    decode_in = [row3(q_s), row3(krow_s), row3(vrow_s), lft_s[0].T[:, :, None],
                 cache_fox_logf[l].transpose(0, 2, 1),
                 cache_fox_k[l].transpose(0, 2, 3, 1).reshape(n_pool, FOX_WIDTH, PAGE_SIZE),
                 cache_fox_v[l].transpose(0, 2, 3, 1),
                 jnp.zeros((bs, 1, FOX_WIDTH), F32)]
    groups_per_seq = n_pages // PAGES_PER_STEP
    n_groups = bs * groups_per_seq
    in_fox = min(_fox_prompt_steps(bp, seq), n_groups)
    in_ffn = min(_ffn_prompt_steps(bp * seq, FFN_ROW_TILE), n_groups - in_fox)
    assert in_fox == _fox_prompt_steps(bp, seq) and in_fox % groups_per_seq == 0
    assert in_ffn == _ffn_prompt_steps(bp * seq, FFN_ROW_TILE) and in_ffn % groups_per_seq == 0
    assert n_groups > in_fox + in_ffn

    xp2 = x_prompt.reshape(bp * seq, D_MODEL)
    q_p, kt_p, ktb_p, vt_p, vb_p, lft_p, ct_p, u_p, qm_p, g_p = _front(xp2, bp, ROW_TILE, False, *front_w)
    wmt = w_mem_kv[l].T.astype(BF16)
    mkt_p, mvt_p, mktb_p, mvb_p = _memkv(mem_prompt.reshape(bp * N_MEM, D_MODEL), bp, g_mem[l][None, :],
                                         wmt, w_mem_kv[l][:, MEM_WIDTH:].astype(BF16))
    yfox_p, decode_in[-1] = _fox_prompt(q_p, ktb_p, vb_p, ct_p.reshape(bp, FOX_WIDTH // LANES, 2, seq), bp, seq,
                                        page_table, decode_in, 0)
    x1_p, h2_p = _tail_prompt(xp2, yfox_p, u_p, qm_p, g_p, mktb_p, mvb_p, seq // ROW_TILE, ROW_TILE, *tail_w)
    ffn_tiles = seq // FFN_ROW_TILE
    y_p, gtail_p, decode_in[-1] = _ffn_prompt(h2_p, x1_p, ffn_tiles, FFN_ROW_TILE, *ffn_w,
                                              page_table, decode_in, in_fox)

    yfox_s = _decode_attn(page_table, decode_in, in_fox + in_ffn, n_groups - 1)
    yc_s = _mem_decode(row3(qm_s),
                       cache_mem_k[l].transpose(0, 2, 3, 1).reshape(bs, MEM_WIDTH, N_MEM),
                       cache_mem_v[l].transpose(0, 2, 3, 1).reshape(bs, MEM_WIDTH, N_MEM))
    x1_s, h2_s = _tail_sample(xs2, yfox_s.reshape(bs, FOX_WIDTH).astype(BF16), u_s,
                              state_conv[l].transpose(1, 0, 2), yc_s.reshape(bs, MEM_WIDTH), g_s, *tail_w)
    ffn_state_c = state_ffn_conv[l].reshape(bs, FFN_CONV_WIDTH - 1, n_chunks, FF_CHUNK).transpose(2, 1, 0, 3)
    y_s, gt_s = _ffn_sample(h2_s, ffn_state_c, x1_s, *ffn_w)

    y_prompt = y_p.reshape(bp, seq, D_MODEL)
    y_sample = y_s.reshape(bs, 1, D_MODEL)
    fox_k_p = _heads_last(kt_p, bp, N_FOX_HEADS, seq)[None]
    fox_v_p = _heads_last(vt_p, bp, N_FOX_HEADS, seq)[None]
    fox_logf_p = lft_p.transpose(0, 2, 1)[None]
    conv_state_p = u_p.reshape(bp, seq, CONV_CH)[:, seq - (CONV_WIDTH - 1):][None]
    gt_last = gtail_p.reshape(bp, ffn_tiles, n_chunks, SUBLANES, FF_CHUNK)[
        :, -1, :, SUBLANES - (FFN_CONV_WIDTH - 1):, :]
    ffn_state_p = gt_last.transpose(0, 2, 1, 3).reshape(1, bp, FFN_CONV_WIDTH - 1, D_FF)
    mem_k_p = _heads_last(mkt_p, bp, N_MEM_HEADS, N_MEM)[None]
    mem_v_p = _heads_last(mvt_p, bp, N_MEM_HEADS, N_MEM)[None]
    fox_k_s = _heads_last(kt_s, 1, N_FOX_HEADS, bs).reshape(1, bs, 1, N_FOX_HEADS, HEAD_DIM)
    fox_v_s = _heads_last(vt_s, 1, N_FOX_HEADS, bs).reshape(1, bs, 1, N_FOX_HEADS, HEAD_DIM)
    fox_logf_s = lft_s[0].T.reshape(1, bs, 1, N_FOX_HEADS)
    conv_state_s = jnp.concatenate([state_conv[l][:, 1:], u_s[:, None, :]], axis=1)[None]
    gt_new = gt_s.transpose(1, 0, 2).reshape(bs, 1, D_FF)
    ffn_state_s = jnp.concatenate([state_ffn_conv[l][:, 1:], gt_new], axis=1)[None]
    return (y_prompt, y_sample, fox_k_p, fox_v_p, fox_logf_p, conv_state_p, ffn_state_p, mem_k_p, mem_v_p,
            fox_k_s, fox_v_s, fox_logf_s, conv_state_s, ffn_state_s)
```

```python
import functools
from typing import Any, NamedTuple

import jax
import jax.numpy as jnp
from jax import lax
from jax.experimental import pallas as pl
from jax.experimental.pallas import tpu as pltpu

F32 = jnp.float32
BF16 = jnp.bfloat16

D_MODEL = 1024
HEAD_DIM = 64
N_FOX_HEADS = 8
FOX_WIDTH = N_FOX_HEADS * HEAD_DIM
CONV_CH = 256
CONV_WIDTH = 31
N_MEM = 256
N_MEM_HEADS = 4
MEM_WIDTH = N_MEM_HEADS * HEAD_DIM
D_FF = 3 * D_MODEL
FFN_CONV_WIDTH = 3
PAGE_SIZE = 128
EPS = 1e-6
ATTN_SCALE = HEAD_DIM ** -0.5
NEG_INF = -1e30

LANES = 128
SUBLANES = 8
BF16_SUBLANES = 16
ROW_TILE = 256
FFN_ROW_TILE = 512
ATTN_BLOCK = 512
CONV_HALO = 32
FFN_HALO = BF16_SUBLANES
FF_CHUNK = 512
PAGES_PER_STEP = 16
FFN_CHUNKS_PER_GROUP = 3
MEM_DECODE_SEQS = 8
VMEM_LIMIT_BYTES = 56 * 1024 * 1024

_NT = (((1,), (1,)), ((), ()))


def _params(n_axes):
    return pltpu.CompilerParams(dimension_semantics=("arbitrary",) * n_axes,
                                vmem_limit_bytes=VMEM_LIMIT_BYTES)


def _resident(shape):
    nd = len(shape)
    return pl.BlockSpec(shape, lambda *_: (0,) * nd, pipeline_mode=pl.Buffered(1))


def _rms_scale(x):
    return x * lax.rsqrt(jnp.mean(x * x, axis=-1, keepdims=True) + EPS)


def _sigmoid(x):
    return 1.0 / (1.0 + jnp.exp(-x))


def _log_sigmoid(x):
    return jnp.minimum(x, 0.0) - jnp.log(1.0 + jnp.exp(-jnp.abs(x)))


def _split3(x):
    hi = x.astype(BF16)
    r1 = x - hi.astype(F32)
    mid = r1.astype(BF16)
    lo = (r1 - mid.astype(F32)).astype(BF16)
    return hi, mid, lo


def _dot3(a, b01):
    return sum(jnp.dot(t, b01, preferred_element_type=F32) for t in _split3(a))


def _dot3_left(a01, b):
    return sum(jnp.dot(a01, t, preferred_element_type=F32) for t in _split3(b))


def _dot3_nt(a01, b):
    return sum(lax.dot_general(a01, t, _NT, preferred_element_type=F32) for t in _split3(b))


def _ones_where(cond):
    return jnp.where(cond, 1.0, 0.0).astype(BF16)


def _head_mask(n_rows, width):
    r = lax.broadcasted_iota(jnp.int32, (n_rows, width), 0)
    c = lax.broadcasted_iota(jnp.int32, (n_rows, width), 1)
    return (c >= r * HEAD_DIM) & (c < (r + 1) * HEAD_DIM)


def _head_rows(row, n_rows, width):
    return jnp.where(_head_mask(n_rows, width), jnp.broadcast_to(row.astype(F32), (n_rows, width)), 0.0)


def _heads_to_row(col, width):
    n_rows = col.shape[0]
    spread = jnp.where(_head_mask(n_rows, width), jnp.broadcast_to(col[:, 0:1], (n_rows, width)), 0.0)
    return jnp.sum(spread, axis=0, keepdims=True)


def _front_kernel(x_ref, gmix_ref, wq_ref, wkvt_ref, wv_ref, wft_ref, bf_ref, wglu_ref, wqm_ref, wgate_ref,
                  q_ref, kt_ref, ktb_ref, vt_ref, vb_ref, lft_ref, ct_ref, u_ref, qm_ref, g_ref, *rest,
                  tiles_per_seq, emit_rows):
    if emit_rows:
        krow_ref, vrow_ref, carry_ref = rest
    else:
        (carry_ref,) = rest
    i = pl.program_id(0)
    tm = x_ref.shape[0]
    h = (_rms_scale(x_ref[...]) * gmix_ref[...]).astype(BF16)

    q_ref[...] = (jnp.dot(h, wq_ref[...], preferred_element_type=F32) * ATTN_SCALE).astype(BF16)
    kvt = lax.dot_general(wkvt_ref[...], h, _NT, preferred_element_type=F32)
    kt = kvt[:FOX_WIDTH]
    kt_ref[0] = kt
    ktb_ref[0] = kt.astype(BF16)
    vt_ref[0] = kvt[FOX_WIDTH:]
    v = jnp.dot(h, wv_ref[...], preferred_element_type=F32)
    vb_ref[...] = v.astype(BF16)
    if emit_rows:
        vrow_ref[...] = v
        krow_ref[...] = lax.dot_general(h, wkvt_ref[0:FOX_WIDTH, :], _NT, preferred_element_type=F32)

    flt = lax.dot_general(wft_ref[...], h, _NT, preferred_element_type=F32)[:N_FOX_HEADS] + bf_ref[...]
    lft = _log_sigmoid(flt)
    lft_ref[0] = lft

    @pl.when(i % tiles_per_seq == 0)
    def _():
        carry_ref[...] = jnp.zeros_like(carry_ref)

    r = lax.broadcasted_iota(jnp.int32, (tm, tm), 0)
    c = lax.broadcasted_iota(jnp.int32, (tm, tm), 1)
    csum = _dot3(lft, _ones_where(r <= c)) + carry_ref[:, 0:1]
    ct_ref[0] = csum
    carry_ref[...] = jnp.broadcast_to(csum[:, tm - 1:tm], carry_ref.shape)

    glu = jnp.dot(h, wglu_ref[...], preferred_element_type=F32)
    u_ref[...] = glu[:, :CONV_CH] * _sigmoid(glu[:, CONV_CH:])
    qm = jnp.dot(h, wqm_ref[...], preferred_element_type=F32)
    qm_ref[...] = (qm * ATTN_SCALE).astype(BF16)
    g_ref[...] = _sigmoid(jnp.dot(h, wgate_ref[...], preferred_element_type=F32)).astype(BF16)


def _front(x2d, batch, tm, emit_rows, *weights):
    n = x2d.shape[0]
    seq = n // batch
    tiles_per_seq = seq // tm
    rows = lambda w: pl.BlockSpec((tm, w), lambda i: (i, 0))
    cols = lambda r: pl.BlockSpec((1, r, tm), lambda i: (i // tiles_per_seq, 0, i % tiles_per_seq))
    out = [
        (jax.ShapeDtypeStruct((n, FOX_WIDTH), BF16), rows(FOX_WIDTH)),
        (jax.ShapeDtypeStruct((batch, FOX_WIDTH, seq), F32), cols(FOX_WIDTH)),
        (jax.ShapeDtypeStruct((batch, FOX_WIDTH, seq), BF16), cols(FOX_WIDTH)),
        (jax.ShapeDtypeStruct((batch, FOX_WIDTH, seq), F32), cols(FOX_WIDTH)),
        (jax.ShapeDtypeStruct((n, FOX_WIDTH), BF16), rows(FOX_WIDTH)),
        (jax.ShapeDtypeStruct((batch, N_FOX_HEADS, seq), F32), cols(N_FOX_HEADS)),
        (jax.ShapeDtypeStruct((batch, N_FOX_HEADS, seq), F32), cols(N_FOX_HEADS)),
        (jax.ShapeDtypeStruct((n, CONV_CH), F32), rows(CONV_CH)),
        (jax.ShapeDtypeStruct((n, MEM_WIDTH), BF16), rows(MEM_WIDTH)),
        (jax.ShapeDtypeStruct((n, 3 * D_MODEL), BF16), rows(3 * D_MODEL)),
    ]
    if emit_rows:
        out += [(jax.ShapeDtypeStruct((n, FOX_WIDTH), F32), rows(FOX_WIDTH))] * 2
    return pl.pallas_call(
        functools.partial(_front_kernel, tiles_per_seq=tiles_per_seq, emit_rows=emit_rows),
        grid=(n // tm,),
        in_specs=[rows(D_MODEL)] + [_resident(w.shape) for w in weights],
        out_specs=[s for _, s in out],
        out_shape=[s for s, _ in out],
        scratch_shapes=[pltpu.VMEM((N_FOX_HEADS, LANES), F32)],
        compiler_params=_params(1),
        name="front_rows" if emit_rows else "front",
    )(x2d, *weights)


def _memkv_kernel(m_ref, g_ref, wt_ref, wv_ref, mkt_ref, mvt_ref, mktb_ref, mvb_ref):
    h = (_rms_scale(m_ref[...]) * g_ref[...]).astype(BF16)
    kvt = lax.dot_general(wt_ref[...], h, _NT, preferred_element_type=F32)
    mkt_ref[0] = kvt[:MEM_WIDTH]
    mvt_ref[0] = kvt[MEM_WIDTH:]
    mktb_ref[0] = kvt[:MEM_WIDTH].astype(BF16)
    mvb_ref[...] = jnp.dot(h, wv_ref[...], preferred_element_type=F32).astype(BF16)


def _memkv(mem2d, batch, gmem, wkvt, wv):
    rows = lambda w: pl.BlockSpec((N_MEM, w), lambda i: (i, 0))
    per_b = pl.BlockSpec((1, MEM_WIDTH, N_MEM), lambda i: (i, 0, 0))
    t_shape = (batch, MEM_WIDTH, N_MEM)
    return pl.pallas_call(
        _memkv_kernel,
        grid=(batch,),
        in_specs=[rows(D_MODEL), _resident(gmem.shape), _resident(wkvt.shape), _resident(wv.shape)],
        out_specs=[per_b, per_b, per_b, rows(MEM_WIDTH)],
        out_shape=[jax.ShapeDtypeStruct(t_shape, F32), jax.ShapeDtypeStruct(t_shape, F32),
                   jax.ShapeDtypeStruct(t_shape, BF16), jax.ShapeDtypeStruct((batch * N_MEM, MEM_WIDTH), BF16)],
        compiler_params=_params(1),
        name="memkv",
    )(mem2d, gmem, wkvt, wv)


def _fox_prompt_kernel(pt_ref, q_ref, kt_ref, v_ref, ct_ref, *refs, first, last):
    o_ref = refs[N_DECODE_INPUTS]
    dec = _decode_refs(pt_ref, refs[:N_DECODE_INPUTS], refs[N_DECODE_INPUTS + 1], refs[N_DECODE_INPUTS + 2:])
    nq = pl.num_programs(2)
    i = pl.program_id(2)
    steps_before = (pl.program_id(0) * pl.num_programs(1) + pl.program_id(1)) * (nq * (nq - 1) // 2) \
        + i * (i - 1) // 2
    tq = q_ref.shape[0]
    tk = tq
    q = q_ref[...]
    lane = lax.broadcasted_iota(jnp.int32, (tq, LANES), 1)
    zero = jnp.zeros((), q.dtype)
    q2 = jnp.concatenate([jnp.where(lane < HEAD_DIM, q, zero), jnp.where(lane >= HEAD_DIM, q, zero)], axis=0)

    def step(j, carry, masked):
        m, l, acc = carry
        if not masked:
            group = first + steps_before + j
            _decode_fetch(dec, group, first, last)
            _decode_compute(dec, group, first)
        start = pl.multiple_of(j * tk, tk)
        s = jnp.dot(q2, kt_ref[0, :, pl.ds(start, tk)], preferred_element_type=F32)
        ck = ct_ref[0, 0, :, pl.ds(start, tk)]
        z = jnp.concatenate([s[:tq] - ck[0:1], s[tq:] - ck[1:2]], axis=0)
        if masked:
            row = lax.broadcasted_iota(jnp.int32, (2 * tq, tk), 0)
            col = lax.broadcasted_iota(jnp.int32, (2 * tq, tk), 1)
            z = jnp.where(col <= (row & (tq - 1)), z, NEG_INF)
        m_new = jnp.maximum(m, jnp.max(z, axis=-1, keepdims=True))
        alpha = jnp.exp(m - m_new)
        p = jnp.exp(z - m_new)
        l = alpha * l + jnp.sum(p, axis=-1, keepdims=True)
        acc = alpha * acc + jnp.dot(p.astype(BF16), v_ref[pl.ds(start, tk), :], preferred_element_type=F32)
        if not masked:
            _decode_finish(dec, group)
        return m_new, l, acc

    init = (jnp.full((2 * tq, 1), NEG_INF, F32), jnp.zeros((2 * tq, 1), F32), jnp.zeros((2 * tq, LANES), F32))
    carry = lax.fori_loop(0, i, functools.partial(step, masked=False), init)
    _, l, acc = step(i, carry, True)
    o = acc / l
    o_ref[...] = jnp.where(lane < HEAD_DIM, o[:tq], o[tq:]).astype(o_ref.dtype)


def _fox_prompt_steps(batch, seq):
    nq = seq // ATTN_BLOCK
    return batch * (FOX_WIDTH // LANES) * (nq * (nq - 1) // 2)


def _fox_prompt(q, ktb, vb, ct, batch, seq, page_table, decode_in, first):
    assert ATTN_BLOCK & (ATTN_BLOCK - 1) == 0
    nq = seq // ATTN_BLOCK
    n_pairs = FOX_WIDTH // LANES
    last = first + _fox_prompt_steps(batch, seq) - 1
    d_args, d_specs, d_out_spec, d_out_shape, d_scratch = _decode_operands(page_table, *decode_in)
    tile = pl.BlockSpec((ATTN_BLOCK, LANES), lambda b, p, i, pt: (b * nq + i, p))
    grid_spec = pltpu.PrefetchScalarGridSpec(
        num_scalar_prefetch=1,
        grid=(batch, n_pairs, nq),
        in_specs=[
            tile,
            pl.BlockSpec((1, LANES, seq), lambda b, p, i, pt: (b, p, 0)),
            pl.BlockSpec((seq, LANES), lambda b, p, i, pt: (b, p)),
            pl.BlockSpec((1, 1, 2, seq), lambda b, p, i, pt: (b, p, 0, 0)),
        ] + d_specs,
        out_specs=[tile, d_out_spec],
        scratch_shapes=d_scratch,
    )
    return pl.pallas_call(
        functools.partial(_fox_prompt_kernel, first=first, last=last),
        grid_spec=grid_spec,
        out_shape=[jax.ShapeDtypeStruct((batch * seq, FOX_WIDTH), BF16), d_out_shape],
        input_output_aliases={4 + len(d_args): 1},
        compiler_params=_params(3),
        name="fox_prompt",
    )(page_table, q, ktb, vb, ct, *d_args)


class _Decode(NamedTuple):
    pt: Any
    q: Any
    knew: Any
    vnew: Any
    lfnew: Any
    tw: Any
    lf_hbm: Any
    k_hbm: Any
    v_hbm: Any
    o_hbm: Any
    lfbuf: Any
    kbuf: Any
    vbuf: Any
    sem: Any
    orow: Any
    osem: Any
    m: Any
    l: Any
    cn: Any
    carry: Any
    acc: Any


N_DECODE_INPUTS = 9
N_DECODE_SCRATCH = 11


def _decode_refs(pt_ref, ins, o_hbm, scratch):
    assert len(ins) == N_DECODE_INPUTS and len(scratch) == N_DECODE_SCRATCH
    return _Decode(pt_ref, *ins[:N_DECODE_INPUTS - 1], o_hbm, *scratch)


def _page_copies(r, group, slot):
    n = PAGES_PER_STEP
    n_groups = r.pt.shape[1] // n
    if group is not None:
        seq_i = group // n_groups
        first = (n_groups - 1 - group % n_groups) * n
    copies = []
    for j in range(n):
        page = 0 if group is None else r.pt[seq_i, first + j]
        copies.append(pltpu.make_async_copy(r.lf_hbm.at[page], r.lfbuf.at[slot, j], r.sem.at[slot, 0, j]))
        copies.append(pltpu.make_async_copy(r.k_hbm.at[page], r.kbuf.at[slot, j], r.sem.at[slot, 1, j]))
        copies.append(pltpu.make_async_copy(r.v_hbm.at[page], r.vbuf.at[slot, j], r.sem.at[slot, 2, j]))
    return copies


def _start_all(copies):
    for idx, cp in enumerate(copies):
        cp.start(priority=idx % 2)


def _decode_fetch(r, group, first, last):
    slot = (group - first) % 2

    @pl.when(group == first)
    def _():
        _start_all(_page_copies(r, first, 0))

    @pl.when(group < last)
    def _():
        _start_all(_page_copies(r, group + 1, 1 - slot))

    for cp in _page_copies(r, None, slot):
        cp.wait()

    n_groups = r.pt.shape[1] // PAGES_PER_STEP
    seq_i = group // n_groups

    @pl.when(group % n_groups == 0)
    def _():
        qf = _head_rows(r.q[seq_i], N_FOX_HEADS, FOX_WIDTH)
        s_new = jnp.sum(qf * r.knew[seq_i], axis=-1, keepdims=True)
        r.m[...] = jnp.broadcast_to(s_new, r.m.shape)
        r.l[...] = jnp.ones_like(r.l)
        r.cn[...] = jnp.ones_like(r.cn)
        r.carry[...] = jnp.broadcast_to(r.lfnew[seq_i], r.carry.shape)
        r.acc[...] = jnp.zeros_like(r.acc)


def _decode_compute(r, group, first):
    n = PAGES_PER_STEP
    n_groups = r.pt.shape[1] // n
    slot = (group - first) % 2
    qb = _head_rows(r.q[group // n_groups], N_FOX_HEADS, FOX_WIDTH).astype(BF16)
    x = r.lfbuf[slot].reshape(n * N_FOX_HEADS, PAGE_SIZE)
    wt = _dot3(x, r.tw[...])
    later = r.carry[...]
    scores = [None] * n
    for j in reversed(range(n)):
        rows = slice(j * N_FOX_HEADS, (j + 1) * N_FOX_HEADS)
        bias = wt[rows, :PAGE_SIZE] + later
        later = later + wt[rows, PAGE_SIZE:]
        scores[j] = jnp.dot(qb, r.kbuf[slot, j].astype(BF16), preferred_element_type=F32) + bias
    r.carry[...] = later
    z = jnp.concatenate(scores, axis=-1)
    m_old = r.m[...]
    m_new = jnp.maximum(m_old, jnp.max(z, axis=-1, keepdims=True))
    alpha = jnp.exp(m_old - m_new)
    p = jnp.exp(z - m_new[:, 0:1])
    r.l[...] = alpha * r.l[...] + jnp.sum(p, axis=-1, keepdims=True)
    r.cn[...] = alpha * r.cn[...]
    r.m[...] = m_new
    for h in range(N_FOX_HEADS):
        a = r.acc[h] * alpha[h:h + 1, :]
        for j in range(n):
            a = a + r.vbuf[slot, j, h] * p[h:h + 1, j * PAGE_SIZE:(j + 1) * PAGE_SIZE]
        r.acc[h] = a


def _decode_finish(r, group):
    n_groups = r.pt.shape[1] // PAGES_PER_STEP
    seq_i = group // n_groups

    @pl.when(group % n_groups == n_groups - 1)
    def _():
        acc2d = r.acc[...].reshape(FOX_WIDTH, PAGE_SIZE)
        ones = jnp.ones((BF16_SUBLANES, PAGE_SIZE), BF16)
        past = _dot3_nt(ones, acc2d)[0:1]
        new = _heads_to_row(r.cn[...], FOX_WIDTH) * r.vnew[seq_i]
        r.orow[...] = (past + new) / _heads_to_row(r.l[...], FOX_WIDTH)
        cp = pltpu.make_async_copy(r.orow, r.o_hbm.at[seq_i], r.osem)
        cp.start()
        cp.wait()


def _decode_operands(page_table, q, knew, vnew, lfnew, lf_cache, k_cache, v_cache, o_prev):
    del page_table
    n = PAGES_PER_STEP
    pos = jnp.arange(PAGE_SIZE)
    tw = jnp.concatenate([_ones_where(pos[:, None] > pos[None, :]),
                          jnp.ones((PAGE_SIZE, PAGE_SIZE), BF16)], axis=1)
    hbm = pl.BlockSpec(memory_space=pl.ANY)
    small = (q, knew, vnew, lfnew, tw)
    args = small + (lf_cache, k_cache, v_cache, o_prev)
    in_specs = [_resident(a.shape) for a in small] + [hbm] * 4
    scratch = ([pltpu.VMEM((2, n, N_FOX_HEADS, PAGE_SIZE), F32),
                pltpu.VMEM((2, n, FOX_WIDTH, PAGE_SIZE), F32),
                pltpu.VMEM((2, n, N_FOX_HEADS, HEAD_DIM, PAGE_SIZE), F32),
                pltpu.SemaphoreType.DMA((2, 3, n)),
                pltpu.VMEM((1, FOX_WIDTH), F32),
                pltpu.SemaphoreType.DMA(())]
               + [pltpu.VMEM((N_FOX_HEADS, LANES), F32)] * 4
               + [pltpu.VMEM((N_FOX_HEADS, HEAD_DIM, PAGE_SIZE), F32)])
    assert len(args) == N_DECODE_INPUTS and len(scratch) == N_DECODE_SCRATCH
    return args, in_specs, hbm, jax.ShapeDtypeStruct(o_prev.shape, o_prev.dtype), scratch


def _mem_decode_kernel(q_ref, mkt_ref, mvt_ref, o_ref):
    for b in range(q_ref.shape[0]):
        qb = _head_rows(q_ref[b], SUBLANES, MEM_WIDTH).astype(BF16)
        s = jnp.dot(qb, mkt_ref[b].astype(BF16), preferred_element_type=F32)
        p = jnp.exp(s - jnp.max(s, axis=-1, keepdims=True))
        l = jnp.sum(p, axis=-1, keepdims=True)
        o = lax.dot_general(p.astype(BF16), mvt_ref[b].astype(BF16), _NT, preferred_element_type=F32) / l
        o = jnp.where(_head_mask(SUBLANES, MEM_WIDTH), o, 0.0)
        o_ref[b] = jnp.sum(o, axis=0, keepdims=True).astype(o_ref.dtype)


def _mem_decode(qm, mkt, mvt):
    bsz = qm.shape[0]
    per_step = MEM_DECODE_SEQS
    return pl.pallas_call(
        _mem_decode_kernel,
        grid=(bsz // per_step,),
        in_specs=[pl.BlockSpec((per_step, 1, MEM_WIDTH), lambda b: (b, 0, 0)),
                  pl.BlockSpec((per_step, MEM_WIDTH, N_MEM), lambda b: (b, 0, 0)),
                  pl.BlockSpec((per_step, MEM_WIDTH, N_MEM), lambda b: (b, 0, 0))],
        out_specs=pl.BlockSpec((per_step, 1, MEM_WIDTH), lambda b: (b, 0, 0)),
        out_shape=jax.ShapeDtypeStruct((bsz, 1, MEM_WIDTH), BF16),
        compiler_params=_params(1),
        name="mem_decode",
    )(qm, mkt, mvt)


def _conv_norm_act(conv, gcn_ref, bcn_ref):
    mu = jnp.mean(conv, axis=-1, keepdims=True)
    d = conv - mu
    var = jnp.mean(d * d, axis=-1, keepdims=True)
    y = d * lax.rsqrt(var + EPS) * gcn_ref[...] + bcn_ref[...]
    return y * _sigmoid(y)


def _merge_project(x, yfox, yb, yc, g_ref, wa_ref, wb_ref, wc_ref, wo_ref, gffn_ref, x1_ref, h2_ref):
    ga = g_ref[:, :D_MODEL].astype(F32)
    gb = g_ref[:, D_MODEL:2 * D_MODEL].astype(F32)
    gc = g_ref[:, 2 * D_MODEL:].astype(F32)
    merged = (ga * jnp.dot(yfox, wa_ref[...], preferred_element_type=F32)
              + gb * jnp.dot(yb.astype(BF16), wb_ref[...], preferred_element_type=F32)
              + gc * jnp.dot(yc, wc_ref[...], preferred_element_type=F32))
    x1 = x + jnp.dot(merged.astype(BF16), wo_ref[...], preferred_element_type=F32)
    x1_ref[...] = x1
    h2_ref[...] = (_rms_scale(x1) * gffn_ref[...]).astype(BF16)


def _tail_prompt_kernel(x_ref, yfox_ref, u_ref, uhalo_ref, qm_ref, g_ref, mkt_ref, mv_ref,
                        wdw_ref, gcn_ref, bcn_ref, wa_ref, wb_ref, wc_ref, wo_ref, gffn_ref,
                        x1_ref, h2_ref, ubuf_ref, ushift_ref, *, tiles_per_seq):
    i = pl.program_id(0)
    tm = x_ref.shape[0]
    halo = uhalo_ref[...]
    ubuf_ref[0:CONV_HALO, :] = jnp.where(i % tiles_per_seq == 0, jnp.zeros_like(halo), halo)
    ubuf_ref[CONV_HALO:, :] = u_ref[...]
    base = CONV_HALO - (CONV_WIDTH - 1)
    reach = CONV_HALO - SUBLANES + tm
    for phase in range(1, SUBLANES):
        ushift_ref[phase, 0:reach, :] = ubuf_ref[phase:phase + reach, :]
    chunk = 64
    convs = []
    for r0 in range(0, tm, chunk):
        acc = jnp.zeros((chunk, CONV_CH), F32)
        for j in range(CONV_WIDTH):
            phase, lo = (base + j) % SUBLANES, (base + j) // SUBLANES * SUBLANES + r0
            rows = ubuf_ref[lo:lo + chunk, :] if phase == 0 else ushift_ref[phase, lo:lo + chunk, :]
            acc = acc + wdw_ref[j:j + 1, :] * rows
        convs.append(acc)
    yb = _conv_norm_act(jnp.concatenate(convs, axis=0), gcn_ref, bcn_ref)

    qm = qm_ref[...]
    lane = lax.broadcasted_iota(jnp.int32, (tm, MEM_WIDTH), 1)
    mkt = mkt_ref[0]
    mv = mv_ref[...]
    yc = jnp.zeros((tm, MEM_WIDTH), F32)
    for h in range(N_MEM_HEADS):
        in_head = (lane >= h * HEAD_DIM) & (lane < (h + 1) * HEAD_DIM)
        qh = jnp.where(in_head, qm, jnp.zeros((), qm.dtype))
        s = jnp.dot(qh, mkt, preferred_element_type=F32)
        p = jnp.exp(s - jnp.max(s, axis=-1, keepdims=True))
        o = jnp.dot(p.astype(BF16), mv, preferred_element_type=F32) / jnp.sum(p, axis=-1, keepdims=True)
        yc = jnp.where(in_head, o, yc)

    _merge_project(x_ref[...], yfox_ref[...], yb, yc.astype(BF16), g_ref,
                   wa_ref, wb_ref, wc_ref, wo_ref, gffn_ref, x1_ref, h2_ref)


def _tail_sample_kernel(x_ref, yfox_ref, u_ref, state_ref, yc_ref, g_ref,
                        wdw_ref, gcn_ref, bcn_ref, wa_ref, wb_ref, wc_ref, wo_ref, gffn_ref,
                        x1_ref, h2_ref):
    conv = wdw_ref[CONV_WIDTH - 1:CONV_WIDTH, :] * u_ref[...]
    for j in range(CONV_WIDTH - 1):
        conv = conv + wdw_ref[j:j + 1, :] * state_ref[j]
    yb = _conv_norm_act(conv, gcn_ref, bcn_ref)
    _merge_project(x_ref[...], yfox_ref[...], yb, yc_ref[...], g_ref,
                   wa_ref, wb_ref, wc_ref, wo_ref, gffn_ref, x1_ref, h2_ref)


def _tail_prompt(x2d, yfox, u, qm, g, mktb, mvb, tiles_per_seq, tm, *ws):
    n = x2d.shape[0]
    rows = lambda w: pl.BlockSpec((tm, w), lambda i: (i, 0))
    halo_blocks = tm // CONV_HALO
    return pl.pallas_call(
        functools.partial(_tail_prompt_kernel, tiles_per_seq=tiles_per_seq),
        grid=(n // tm,),
        in_specs=[rows(D_MODEL), rows(FOX_WIDTH), rows(CONV_CH),
                  pl.BlockSpec((CONV_HALO, CONV_CH), lambda i: (jnp.maximum(i * halo_blocks - 1, 0), 0)),
                  rows(MEM_WIDTH), rows(3 * D_MODEL),
                  pl.BlockSpec((1, MEM_WIDTH, N_MEM), lambda i: (i // tiles_per_seq, 0, 0)),
                  pl.BlockSpec((N_MEM, MEM_WIDTH), lambda i: (i // tiles_per_seq, 0))]
                 + [_resident(w.shape) for w in ws],
        out_specs=[rows(D_MODEL), rows(D_MODEL)],
        out_shape=[jax.ShapeDtypeStruct((n, D_MODEL), F32), jax.ShapeDtypeStruct((n, D_MODEL), BF16)],
        scratch_shapes=[pltpu.VMEM((CONV_HALO + tm, CONV_CH), F32),
                        pltpu.VMEM((SUBLANES, CONV_HALO + tm, CONV_CH), F32)],
        compiler_params=_params(1),
        name="tail_prompt",
    )(x2d, yfox, u, u, qm, g, mktb, mvb, *ws)


def _tail_sample(x2d, yfox, u, state_t, yc, g, *ws):
    n = x2d.shape[0]
    full = lambda a: pl.BlockSpec(a.shape, lambda i: (0,) * a.ndim)
    ins = (x2d, yfox, u, state_t, yc, g)
    return pl.pallas_call(
        _tail_sample_kernel,
        grid=(1,),
        in_specs=[full(a) for a in ins] + [_resident(w.shape) for w in ws],
        out_specs=[pl.BlockSpec((n, D_MODEL), lambda i: (0, 0))] * 2,
        out_shape=[jax.ShapeDtypeStruct((n, D_MODEL), F32), jax.ShapeDtypeStruct((n, D_MODEL), BF16)],
        compiler_params=_params(1),
        name="tail_sample",
    )(*ins, *ws)


def _ffn_finish(x1_ref, acc_ref, gfin_ref, y_ref):
    x2 = x1_ref[...] + acc_ref[...]
    y_ref[...] = _rms_scale(x2) * gfin_ref[...]


def _ffn_prompt_kernel(pt_ref, h2_ref, halo_ref, x1_ref, wg_ref, wu_ref, wd_ref, wcv_ref, gfin_ref, *refs,
                       tiles_per_seq, first, last):
    y_ref, gtail_ref, o_hbm, hcat_ref, gbuf_ref, acc_ref = refs[N_DECODE_INPUTS:N_DECODE_INPUTS + 6]
    dec = _decode_refs(pt_ref, refs[:N_DECODE_INPUTS], o_hbm, refs[N_DECODE_INPUTS + 6:])
    n_chunks = wg_ref.shape[0]
    hosted_per_tile = n_chunks // FFN_CHUNKS_PER_GROUP
    i = pl.program_id(0)
    tm = h2_ref.shape[0]
    halo = halo_ref[...]
    hcat_ref[0:FFN_HALO, :] = jnp.where(i % tiles_per_seq == 0, jnp.zeros_like(halo), halo)
    hcat_ref[FFN_HALO:, :] = h2_ref[...]
    acc_ref[...] = jnp.zeros_like(acc_ref)

    def chunk(c, group):
        if group is not None:
            _decode_fetch(dec, group, first, last)
            _decode_compute(dec, group, first)
        gbuf_ref[...] = jnp.dot(hcat_ref[...], wg_ref[c], preferred_element_type=F32)
        up = jnp.dot(h2_ref[...], wu_ref[c], preferred_element_type=F32)
        w = wcv_ref[c]
        cv = (w[0:1, :] * gbuf_ref[FFN_HALO - 2:FFN_HALO - 2 + tm, :]
              + w[1:2, :] * gbuf_ref[FFN_HALO - 1:FFN_HALO - 1 + tm, :]
              + w[2:3, :] * gbuf_ref[FFN_HALO:FFN_HALO + tm, :])
        act = (cv * _sigmoid(cv) * up).astype(BF16)
        acc_ref[...] += jnp.dot(act, wd_ref[c], preferred_element_type=F32)
        gtail_ref[0, c] = gbuf_ref[FFN_HALO + tm - SUBLANES:FFN_HALO + tm, :]
        if group is not None:
            _decode_finish(dec, group)

    def plain_chunk(c, carry):
        chunk(c, None)
        return carry

    for k in range(hosted_per_tile):
        c0 = k * FFN_CHUNKS_PER_GROUP
        chunk(c0, first + i * hosted_per_tile + k)
        lax.fori_loop(c0 + 1, c0 + FFN_CHUNKS_PER_GROUP, plain_chunk, 0)
    _ffn_finish(x1_ref, acc_ref, gfin_ref, y_ref)


def _ffn_sample_kernel(h2_ref, state_ref, x1_ref, wg_ref, wu_ref, wd_ref, wcv_ref, gfin_ref,
                       y_ref, gt_ref, acc_ref):
    acc_ref[...] = jnp.zeros_like(acc_ref)

    def chunk(c, carry):
        h2 = h2_ref[...]
        gt = jnp.dot(h2, wg_ref[c], preferred_element_type=F32)
        up = jnp.dot(h2, wu_ref[c], preferred_element_type=F32)
        w = wcv_ref[c]
        cv = w[0:1, :] * state_ref[c, 0] + w[1:2, :] * state_ref[c, 1] + w[2:3, :] * gt
        act = (cv * _sigmoid(cv) * up).astype(BF16)
        acc_ref[...] += jnp.dot(act, wd_ref[c], preferred_element_type=F32)
        gt_ref[c] = gt
        return carry

    lax.fori_loop(0, wg_ref.shape[0], chunk, 0)
    _ffn_finish(x1_ref, acc_ref, gfin_ref, y_ref)


def _ffn_prompt_steps(n_rows, tm):
    assert (D_FF // FF_CHUNK) % FFN_CHUNKS_PER_GROUP == 0
    return (n_rows // tm) * (D_FF // FF_CHUNK // FFN_CHUNKS_PER_GROUP)


def _ffn_prompt(h2, x1, tiles_per_seq, tm, wg, wu, wd, wcv, gfin, page_table, decode_in, first):
    n = h2.shape[0]
    nt = n // tm
    n_chunks = wg.shape[0]
    last = first + _ffn_prompt_steps(n, tm) - 1
    d_args, d_specs, d_out_spec, d_out_shape, d_scratch = _decode_operands(page_table, *decode_in)
    rows = lambda w: pl.BlockSpec((tm, w), lambda i, pt: (i, 0))
    halo_blocks = tm // FFN_HALO
    grid_spec = pltpu.PrefetchScalarGridSpec(
        num_scalar_prefetch=1,
        grid=(nt,),
        in_specs=[rows(D_MODEL),
                  pl.BlockSpec((FFN_HALO, D_MODEL), lambda i, pt: (jnp.maximum(i * halo_blocks - 1, 0), 0)),
                  rows(D_MODEL), _resident(wg.shape), _resident(wu.shape), _resident(wd.shape),
                  _resident(wcv.shape), _resident(gfin.shape)] + d_specs,
        out_specs=[rows(D_MODEL), pl.BlockSpec((1, n_chunks, SUBLANES, FF_CHUNK), lambda i, pt: (i, 0, 0, 0)),
                   d_out_spec],
        scratch_shapes=[pltpu.VMEM((FFN_HALO + tm, D_MODEL), BF16),
                        pltpu.VMEM((FFN_HALO + tm, FF_CHUNK), F32),
                        pltpu.VMEM((tm, D_MODEL), F32)] + d_scratch,
    )
    return pl.pallas_call(
        functools.partial(_ffn_prompt_kernel, tiles_per_seq=tiles_per_seq, first=first, last=last),
        grid_spec=grid_spec,
        out_shape=[jax.ShapeDtypeStruct((n, D_MODEL), F32),
                   jax.ShapeDtypeStruct((nt, n_chunks, SUBLANES, FF_CHUNK), F32), d_out_shape],
        input_output_aliases={8 + len(d_args): 2},
        compiler_params=_params(1),
        name="ffn_prompt",
    )(page_table, h2, h2, x1, wg, wu, wd, wcv, gfin, *d_args)


def _ffn_sample(h2, state_c, x1, wg, wu, wd, wcv, gfin):
    n = h2.shape[0]
    n_chunks = wg.shape[0]
    full = lambda a: pl.BlockSpec(a.shape, lambda i: (0,) * a.ndim)
    return pl.pallas_call(
        _ffn_sample_kernel,
        grid=(1,),
        in_specs=[full(h2), full(state_c), full(x1), _resident(wg.shape), _resident(wu.shape),
                  _resident(wd.shape), _resident(wcv.shape), _resident(gfin.shape)],
        out_specs=[pl.BlockSpec((n, D_MODEL), lambda i: (0, 0)),
                   pl.BlockSpec((n_chunks, n, FF_CHUNK), lambda i: (0, 0, 0))],
        out_shape=[jax.ShapeDtypeStruct((n, D_MODEL), F32),
                   jax.ShapeDtypeStruct((n_chunks, n, FF_CHUNK), F32)],
        scratch_shapes=[pltpu.VMEM((n, D_MODEL), F32)],
        compiler_params=_params(1),
        name="ffn_sample",
    )(h2, state_c, x1, wg, wu, wd, wcv, gfin)


def _pad_rows(a, rows):
    return jnp.concatenate([a, jnp.zeros((rows - a.shape[0],) + a.shape[1:], a.dtype)], axis=0)


def _heads_last(xt, batch, n_heads, n_pos):
    return xt.reshape(batch, n_heads, HEAD_DIM, n_pos).transpose(0, 3, 1, 2)


def kernel(x_prompt, x_sample, cache_fox_k, cache_fox_v, cache_fox_logf, state_conv, state_ffn_conv, cache_mem_k, cache_mem_v, page_table, mem_prompt, g_mix, w_in, b_f, w_conv_dw, g_conv_norm, b_conv_norm, g_mem, w_mem_kv, w_a, w_b, w_c, w_o, g_ffn, w_ffn_in, w_ffn_conv, w_down, g_final):
    depth = g_mix.shape[0]
    assert depth == 1, "kernel is written for the single-layer trunk the problem states"
    bp, seq, _ = x_prompt.shape
    bs, dec_seq, _ = x_sample.shape
    assert dec_seq == 1 and seq % FFN_ROW_TILE == 0 and seq % ATTN_BLOCK == 0 and bs % LANES == 0
    n_pool = cache_fox_k.shape[1]
    n_pages = page_table.shape[1]
    assert n_pages % PAGES_PER_STEP == 0 and cache_fox_k.shape[2] == PAGE_SIZE
    l = 0

    w = w_in[l]
    wt = w.T
    o_f = 3 * FOX_WIDTH
    o_glu = o_f + N_FOX_HEADS
    o_qm = o_glu + 2 * CONV_CH
    o_gate = o_qm + MEM_WIDTH
    front_w = (
        g_mix[l][None, :],
        w[:, :FOX_WIDTH].astype(BF16),
        wt[FOX_WIDTH:o_f].astype(BF16),
        w[:, 2 * FOX_WIDTH:o_f].astype(BF16),
        _pad_rows(wt[o_f:o_glu], BF16_SUBLANES).astype(BF16),
        b_f[l][:, None],
        w[:, o_glu:o_qm].astype(BF16),
        w[:, o_qm:o_gate].astype(BF16),
        w[:, o_gate:].astype(BF16),
    )
    tail_w = (_pad_rows(w_conv_dw[l], CONV_HALO), g_conv_norm[l][None, :], b_conv_norm[l][None, :],
              w_a[l].astype(BF16), w_b[l].astype(BF16), w_c[l].astype(BF16), w_o[l].astype(BF16),
              g_ffn[l][None, :])
    n_chunks = D_FF // FF_CHUNK
    wfi = w_ffn_in[l]
    wg = wfi[:, :D_FF].reshape(D_MODEL, n_chunks, FF_CHUNK).transpose(1, 0, 2).astype(BF16)
    wu = wfi[:, D_FF:].reshape(D_MODEL, n_chunks, FF_CHUNK).transpose(1, 0, 2).astype(BF16)
    wd = w_down[l].reshape(n_chunks, FF_CHUNK, D_MODEL).astype(BF16)
    wcv = _pad_rows(w_ffn_conv[l], SUBLANES).reshape(SUBLANES, n_chunks, FF_CHUNK).transpose(1, 0, 2)
    ffn_w = (wg, wu, wd, wcv, g_final[None, :])

    xs2 = x_sample.reshape(bs, D_MODEL)
    (q_s, kt_s, _, vt_s, _, lft_s, _, u_s, qm_s, g_s, krow_s, vrow_s) = _front(xs2, 1, bs, True, *front_w)
    row3 = lambda a: a.reshape(bs, 1, a.shape[-1])
    decode_in = [row3(q_s), row3(krow_s), row3(vrow_s), lft_s[0].T[:, :, None],
                 cache_fox_logf[l].transpose(0, 2, 1),
                 cache_fox_k[l].transpose(0, 2, 3, 1).reshape(n_pool, FOX_WIDTH, PAGE_SIZE),
                 cache_fox_v[l].transpose(0, 2, 3, 1),
                 jnp.zeros((bs, 1, FOX_WIDTH), F32)]
    groups_per_seq = n_pages // PAGES_PER_STEP
    in_fox = _fox_prompt_steps(bp, seq)
    in_ffn = _ffn_prompt_steps(bp * seq, FFN_ROW_TILE)
    assert in_fox % groups_per_seq == 0 and in_fox + in_ffn == bs * groups_per_seq, \
        "the prompt kernels' steps must cover the decode page groups exactly"

    xp2 = x_prompt.reshape(bp * seq, D_MODEL)
    q_p, kt_p, ktb_p, vt_p, vb_p, lft_p, ct_p, u_p, qm_p, g_p = _front(xp2, bp, ROW_TILE, False, *front_w)
    wmt = w_mem_kv[l].T.astype(BF16)
    mkt_p, mvt_p, mktb_p, mvb_p = _memkv(mem_prompt.reshape(bp * N_MEM, D_MODEL), bp, g_mem[l][None, :],
                                         wmt, w_mem_kv[l][:, MEM_WIDTH:].astype(BF16))
    yfox_p, decode_in[-1] = _fox_prompt(q_p, ktb_p, vb_p, ct_p.reshape(bp, FOX_WIDTH // LANES, 2, seq), bp, seq,
                                        page_table, decode_in, 0)
    x1_p, h2_p = _tail_prompt(xp2, yfox_p, u_p, qm_p, g_p, mktb_p, mvb_p, seq // ROW_TILE, ROW_TILE, *tail_w)
    ffn_tiles = seq // FFN_ROW_TILE
    y_p, gtail_p, yfox_s = _ffn_prompt(h2_p, x1_p, ffn_tiles, FFN_ROW_TILE, *ffn_w,
                                       page_table, decode_in, in_fox)

    yc_s = _mem_decode(row3(qm_s),
                       cache_mem_k[l].transpose(0, 2, 3, 1).reshape(bs, MEM_WIDTH, N_MEM),
                       cache_mem_v[l].transpose(0, 2, 3, 1).reshape(bs, MEM_WIDTH, N_MEM))
    x1_s, h2_s = _tail_sample(xs2, yfox_s.reshape(bs, FOX_WIDTH).astype(BF16), u_s,
                              state_conv[l].transpose(1, 0, 2), yc_s.reshape(bs, MEM_WIDTH), g_s, *tail_w)
    ffn_state_c = state_ffn_conv[l].reshape(bs, FFN_CONV_WIDTH - 1, n_chunks, FF_CHUNK).transpose(2, 1, 0, 3)
    y_s, gt_s = _ffn_sample(h2_s, ffn_state_c, x1_s, *ffn_w)

    y_prompt = y_p.reshape(bp, seq, D_MODEL)
    y_sample = y_s.reshape(bs, 1, D_MODEL)
    fox_k_p = _heads_last(kt_p, bp, N_FOX_HEADS, seq)[None]
    fox_v_p = _heads_last(vt_p, bp, N_FOX_HEADS, seq)[None]
    fox_logf_p = lft_p.transpose(0, 2, 1)[None]
    conv_state_p = u_p.reshape(bp, seq, CONV_CH)[:, seq - (CONV_WIDTH - 1):][None]
    gt_last = gtail_p.reshape(bp, ffn_tiles, n_chunks, SUBLANES, FF_CHUNK)[
        :, -1, :, SUBLANES - (FFN_CONV_WIDTH - 1):, :]
    ffn_state_p = gt_last.transpose(0, 2, 1, 3).reshape(1, bp, FFN_CONV_WIDTH - 1, D_FF)
    mem_k_p = _heads_last(mkt_p, bp, N_MEM_HEADS, N_MEM)[None]
    mem_v_p = _heads_last(mvt_p, bp, N_MEM_HEADS, N_MEM)[None]
    fox_k_s = _heads_last(kt_s, 1, N_FOX_HEADS, bs).reshape(1, bs, 1, N_FOX_HEADS, HEAD_DIM)
    fox_v_s = _heads_last(vt_s, 1, N_FOX_HEADS, bs).reshape(1, bs, 1, N_FOX_HEADS, HEAD_DIM)
    fox_logf_s = lft_s[0].T.reshape(1, bs, 1, N_FOX_HEADS)
    conv_state_s = jnp.concatenate([state_conv[l][:, 1:], u_s[:, None, :]], axis=1)[None]
    gt_new = gt_s.transpose(1, 0, 2).reshape(bs, 1, D_FF)
    ffn_state_s = jnp.concatenate([state_ffn_conv[l][:, 1:], gt_new], axis=1)[None]
    return (y_prompt, y_sample, fox_k_p, fox_v_p, fox_logf_p, conv_state_p, ffn_state_p, mem_k_p, mem_v_p,
            fox_k_s, fox_v_s, fox_logf_s, conv_state_s, ffn_state_s)
```

```python
import functools
from typing import Any, NamedTuple

import jax
import jax.numpy as jnp
from jax import lax
from jax.experimental import pallas as pl
from jax.experimental.pallas import tpu as pltpu

F32 = jnp.float32
BF16 = jnp.bfloat16

D_MODEL = 1024
HEAD_DIM = 64
N_FOX_HEADS = 8
FOX_WIDTH = N_FOX_HEADS * HEAD_DIM
CONV_CH = 256
CONV_WIDTH = 31
N_MEM = 256
N_MEM_HEADS = 4
MEM_WIDTH = N_MEM_HEADS * HEAD_DIM
D_FF = 3 * D_MODEL
FFN_CONV_WIDTH = 3
PAGE_SIZE = 128
EPS = 1e-6
ATTN_SCALE = HEAD_DIM ** -0.5
NEG_INF = -1e30

LANES = 128
SUBLANES = 8
BF16_SUBLANES = 16
FRONT_ROW_TILE = 512
ROW_TILE = 256
FFN_ROW_TILE = 512
ATTN_BLOCK = 512
CONV_HALO = 32
FFN_HALO = BF16_SUBLANES
FF_CHUNK = 512
PAGES_PER_STEP = 16
FFN_CHUNKS_PER_GROUP = 3
MEM_DECODE_SEQS = 8
VMEM_LIMIT_BYTES = 56 * 1024 * 1024

_NT = (((1,), (1,)), ((), ()))


def _params(n_axes):
    return pltpu.CompilerParams(dimension_semantics=("arbitrary",) * n_axes,
                                vmem_limit_bytes=VMEM_LIMIT_BYTES)


def _resident(shape):
    nd = len(shape)
    return pl.BlockSpec(shape, lambda *_: (0,) * nd, pipeline_mode=pl.Buffered(1))


def _rms_scale(x):
    return x * lax.rsqrt(jnp.mean(x * x, axis=-1, keepdims=True) + EPS)


def _sigmoid(x):
    return 1.0 / (1.0 + jnp.exp(-x))


def _log_sigmoid(x):
    return jnp.minimum(x, 0.0) - jnp.log(1.0 + jnp.exp(-jnp.abs(x)))


def _split3(x):
    hi = x.astype(BF16)
    r1 = x - hi.astype(F32)
    mid = r1.astype(BF16)
    lo = (r1 - mid.astype(F32)).astype(BF16)
    return hi, mid, lo


def _dot3(a, b01):
    return sum(jnp.dot(t, b01, preferred_element_type=F32) for t in _split3(a))


def _dot3_left(a01, b):
    return sum(jnp.dot(a01, t, preferred_element_type=F32) for t in _split3(b))


def _dot3_nt(a01, b):
    return sum(lax.dot_general(a01, t, _NT, preferred_element_type=F32) for t in _split3(b))


def _ones_where(cond):
    return jnp.where(cond, 1.0, 0.0).astype(BF16)


def _head_mask(n_rows, width):
    r = lax.broadcasted_iota(jnp.int32, (n_rows, width), 0)
    c = lax.broadcasted_iota(jnp.int32, (n_rows, width), 1)
    return (c >= r * HEAD_DIM) & (c < (r + 1) * HEAD_DIM)


def _head_rows(row, n_rows, width):
    return jnp.where(_head_mask(n_rows, width), jnp.broadcast_to(row.astype(F32), (n_rows, width)), 0.0)


def _heads_to_row(col, width):
    n_rows = col.shape[0]
    spread = jnp.where(_head_mask(n_rows, width), jnp.broadcast_to(col[:, 0:1], (n_rows, width)), 0.0)
    return jnp.sum(spread, axis=0, keepdims=True)


def _front_kernel(x_ref, gmix_ref, wq_ref, wkvt_ref, wv_ref, wft_ref, bf_ref, wglu_ref, wqm_ref, wgate_ref,
                  q_ref, kt_ref, ktb_ref, vt_ref, vb_ref, lft_ref, ct_ref, u_ref, qm_ref, g_ref, *rest,
                  tiles_per_seq, emit_rows):
    if emit_rows:
        krow_ref, vrow_ref, carry_ref = rest
    else:
        (carry_ref,) = rest
    i = pl.program_id(0)
    tm = x_ref.shape[0]
    h = (_rms_scale(x_ref[...]) * gmix_ref[...]).astype(BF16)

    q_ref[...] = (jnp.dot(h, wq_ref[...], preferred_element_type=F32) * ATTN_SCALE).astype(BF16)
    kvt = lax.dot_general(wkvt_ref[...], h, _NT, preferred_element_type=F32)
    kt = kvt[:FOX_WIDTH]
    kt_ref[0] = kt
    ktb_ref[0] = kt.astype(BF16)
    vt_ref[0] = kvt[FOX_WIDTH:]
    v = jnp.dot(h, wv_ref[...], preferred_element_type=F32)
    vb_ref[...] = v.astype(BF16)
    if emit_rows:
        vrow_ref[...] = v
        krow_ref[...] = lax.dot_general(h, wkvt_ref[0:FOX_WIDTH, :], _NT, preferred_element_type=F32)

    flt = lax.dot_general(wft_ref[...], h, _NT, preferred_element_type=F32)[:N_FOX_HEADS] + bf_ref[...]
    lft = _log_sigmoid(flt)
    lft_ref[0] = lft

    @pl.when(i % tiles_per_seq == 0)
    def _():
        carry_ref[...] = jnp.zeros_like(carry_ref)

    r = lax.broadcasted_iota(jnp.int32, (tm, tm), 0)
    c = lax.broadcasted_iota(jnp.int32, (tm, tm), 1)
    csum = _dot3(lft, _ones_where(r <= c)) + carry_ref[:, 0:1]
    ct_ref[0] = csum
    carry_ref[...] = jnp.broadcast_to(csum[:, tm - 1:tm], carry_ref.shape)

    glu = jnp.dot(h, wglu_ref[...], preferred_element_type=F32)
    u_ref[...] = glu[:, :CONV_CH] * _sigmoid(glu[:, CONV_CH:])
    qm = jnp.dot(h, wqm_ref[...], preferred_element_type=F32)
    qm_ref[...] = (qm * ATTN_SCALE).astype(BF16)
    g_ref[...] = _sigmoid(jnp.dot(h, wgate_ref[...], preferred_element_type=F32)).astype(BF16)


def _front(x2d, batch, tm, emit_rows, *weights):
    n = x2d.shape[0]
    seq = n // batch
    tiles_per_seq = seq // tm
    rows = lambda w: pl.BlockSpec((tm, w), lambda i: (i, 0))
    cols = lambda r: pl.BlockSpec((1, r, tm), lambda i: (i // tiles_per_seq, 0, i % tiles_per_seq))
    out = [
        (jax.ShapeDtypeStruct((n, FOX_WIDTH), BF16), rows(FOX_WIDTH)),
        (jax.ShapeDtypeStruct((batch, FOX_WIDTH, seq), F32), cols(FOX_WIDTH)),
        (jax.ShapeDtypeStruct((batch, FOX_WIDTH, seq), BF16), cols(FOX_WIDTH)),
        (jax.ShapeDtypeStruct((batch, FOX_WIDTH, seq), F32), cols(FOX_WIDTH)),
        (jax.ShapeDtypeStruct((n, FOX_WIDTH), BF16), rows(FOX_WIDTH)),
        (jax.ShapeDtypeStruct((batch, N_FOX_HEADS, seq), F32), cols(N_FOX_HEADS)),
        (jax.ShapeDtypeStruct((batch, N_FOX_HEADS, seq), F32), cols(N_FOX_HEADS)),
        (jax.ShapeDtypeStruct((n, CONV_CH), F32), rows(CONV_CH)),
        (jax.ShapeDtypeStruct((n, MEM_WIDTH), BF16), rows(MEM_WIDTH)),
        (jax.ShapeDtypeStruct((n, 3 * D_MODEL), BF16), rows(3 * D_MODEL)),
    ]
    if emit_rows:
        out += [(jax.ShapeDtypeStruct((n, FOX_WIDTH), F32), rows(FOX_WIDTH))] * 2
    return pl.pallas_call(
        functools.partial(_front_kernel, tiles_per_seq=tiles_per_seq, emit_rows=emit_rows),
        grid=(n // tm,),
        in_specs=[rows(D_MODEL)] + [_resident(w.shape) for w in weights],
        out_specs=[s for _, s in out],
        out_shape=[s for s, _ in out],
        scratch_shapes=[pltpu.VMEM((N_FOX_HEADS, LANES), F32)],
        compiler_params=_params(1),
        name="front_rows" if emit_rows else "front",
    )(x2d, *weights)


def _memkv_kernel(m_ref, g_ref, wt_ref, wv_ref, mkt_ref, mvt_ref, mktb_ref, mvb_ref):
    h = (_rms_scale(m_ref[...]) * g_ref[...]).astype(BF16)
    kvt = lax.dot_general(wt_ref[...], h, _NT, preferred_element_type=F32)
    mkt_ref[0] = kvt[:MEM_WIDTH]
    mvt_ref[0] = kvt[MEM_WIDTH:]
    mktb_ref[0] = kvt[:MEM_WIDTH].astype(BF16)
    mvb_ref[...] = jnp.dot(h, wv_ref[...], preferred_element_type=F32).astype(BF16)


def _memkv(mem2d, batch, gmem, wkvt, wv):
    rows = lambda w: pl.BlockSpec((N_MEM, w), lambda i: (i, 0))
    per_b = pl.BlockSpec((1, MEM_WIDTH, N_MEM), lambda i: (i, 0, 0))
    t_shape = (batch, MEM_WIDTH, N_MEM)
    return pl.pallas_call(
        _memkv_kernel,
        grid=(batch,),
        in_specs=[rows(D_MODEL), _resident(gmem.shape), _resident(wkvt.shape), _resident(wv.shape)],
        out_specs=[per_b, per_b, per_b, rows(MEM_WIDTH)],
        out_shape=[jax.ShapeDtypeStruct(t_shape, F32), jax.ShapeDtypeStruct(t_shape, F32),
                   jax.ShapeDtypeStruct(t_shape, BF16), jax.ShapeDtypeStruct((batch * N_MEM, MEM_WIDTH), BF16)],
        compiler_params=_params(1),
        name="memkv",
    )(mem2d, gmem, wkvt, wv)


def _fox_prompt_kernel(pt_ref, q_ref, kt_ref, v_ref, ct_ref, *refs, first, last):
    o_ref = refs[N_DECODE_INPUTS]
    dec = _decode_refs(pt_ref, refs[:N_DECODE_INPUTS], refs[N_DECODE_INPUTS + 1], refs[N_DECODE_INPUTS + 2:])
    nq = pl.num_programs(2)
    i = pl.program_id(2)
    steps_before = (pl.program_id(0) * pl.num_programs(1) + pl.program_id(1)) * (nq * (nq - 1) // 2) \
        + i * (i - 1) // 2
    tq = q_ref.shape[0]
    tk = tq
    q = q_ref[...]
    lane = lax.broadcasted_iota(jnp.int32, (tq, LANES), 1)
    zero = jnp.zeros((), q.dtype)
    q2 = jnp.concatenate([jnp.where(lane < HEAD_DIM, q, zero), jnp.where(lane >= HEAD_DIM, q, zero)], axis=0)

    ones = jnp.ones((tk, LANES), BF16)

    def step(j, carry, masked):
        m, accl = carry
        if not masked:
            group = first + steps_before + j
            _decode_fetch(dec, group, first, last)
            _decode_compute(dec, group, first)
        start = pl.multiple_of(j * tk, tk)
        s = jnp.dot(q2, kt_ref[0, :, pl.ds(start, tk)], preferred_element_type=F32)
        ck = ct_ref[0, 0, :, pl.ds(start, tk)]
        z = jnp.concatenate([s[:tq] - ck[0:1], s[tq:] - ck[1:2]], axis=0)
        if masked:
            row = lax.broadcasted_iota(jnp.int32, (2 * tq, tk), 0)
            col = lax.broadcasted_iota(jnp.int32, (2 * tq, tk), 1)
            z = jnp.where(col <= (row & (tq - 1)), z, NEG_INF)
        m_new = jnp.maximum(m, jnp.max(z, axis=-1, keepdims=True))
        alpha = jnp.exp(m - m_new)
        p = jnp.exp((z - m_new).astype(BF16))
        v1 = jnp.concatenate([v_ref[pl.ds(start, tk), :], ones], axis=1)
        accl = alpha * accl + jnp.dot(p, v1, preferred_element_type=F32)
        if not masked:
            _decode_finish(dec, group)
        return m_new, accl

    init = (jnp.full((2 * tq, 1), NEG_INF, F32), jnp.zeros((2 * tq, 2 * LANES), F32))
    carry = lax.fori_loop(0, i, functools.partial(step, masked=False), init)
    _, accl = step(i, carry, True)
    o = accl[:, :LANES] / accl[:, LANES:]
    o_ref[...] = jnp.where(lane < HEAD_DIM, o[:tq], o[tq:]).astype(o_ref.dtype)


def _fox_prompt_steps(batch, seq):
    nq = seq // ATTN_BLOCK
    return batch * (FOX_WIDTH // LANES) * (nq * (nq - 1) // 2)


def _fox_prompt(q, ktb, vb, ct, batch, seq, page_table, decode_in, first):
    assert ATTN_BLOCK & (ATTN_BLOCK - 1) == 0
    nq = seq // ATTN_BLOCK
    n_pairs = FOX_WIDTH // LANES
    last = first + _fox_prompt_steps(batch, seq) - 1
    d_args, d_specs, d_out_spec, d_out_shape, d_scratch = _decode_operands(page_table, *decode_in)
    tile = pl.BlockSpec((ATTN_BLOCK, LANES), lambda b, p, i, pt: (b * nq + i, p))
    grid_spec = pltpu.PrefetchScalarGridSpec(
        num_scalar_prefetch=1,
        grid=(batch, n_pairs, nq),
        in_specs=[
            tile,
            pl.BlockSpec((1, LANES, seq), lambda b, p, i, pt: (b, p, 0)),
            pl.BlockSpec((seq, LANES), lambda b, p, i, pt: (b, p)),
            pl.BlockSpec((1, 1, 2, seq), lambda b, p, i, pt: (b, p, 0, 0)),
        ] + d_specs,
        out_specs=[tile, d_out_spec],
        scratch_shapes=d_scratch,
    )
    return pl.pallas_call(
        functools.partial(_fox_prompt_kernel, first=first, last=last),
        grid_spec=grid_spec,
        out_shape=[jax.ShapeDtypeStruct((batch * seq, FOX_WIDTH), BF16), d_out_shape],
        input_output_aliases={4 + len(d_args): 1},
        compiler_params=_params(3),
        name="fox_prompt",
    )(page_table, q, ktb, vb, ct, *d_args)


class _Decode(NamedTuple):
    pt: Any
    q: Any
    knew: Any
    vnew: Any
    lfnew: Any
    tw: Any
    lf_hbm: Any
    k_hbm: Any
    v_hbm: Any
    o_hbm: Any
    lfbuf: Any
    kbuf: Any
    vbuf: Any
    sem: Any
    orow: Any
    osem: Any
    m: Any
    l: Any
    cn: Any
    carry: Any
    acc: Any


N_DECODE_INPUTS = 9
N_DECODE_SCRATCH = 11


def _decode_refs(pt_ref, ins, o_hbm, scratch):
    assert len(ins) == N_DECODE_INPUTS and len(scratch) == N_DECODE_SCRATCH
    return _Decode(pt_ref, *ins[:N_DECODE_INPUTS - 1], o_hbm, *scratch)


def _page_copies(r, group, slot):
    n = PAGES_PER_STEP
    n_groups = r.pt.shape[1] // n
    if group is not None:
        seq_i = group // n_groups
        first = (n_groups - 1 - group % n_groups) * n
    copies = []
    for j in range(n):
        page = 0 if group is None else r.pt[seq_i, first + j]
        copies.append(pltpu.make_async_copy(r.lf_hbm.at[page], r.lfbuf.at[slot, j], r.sem.at[slot, 0, j]))
        copies.append(pltpu.make_async_copy(r.k_hbm.at[page], r.kbuf.at[slot, j], r.sem.at[slot, 1, j]))
        copies.append(pltpu.make_async_copy(r.v_hbm.at[page], r.vbuf.at[slot, j], r.sem.at[slot, 2, j]))
    return copies


def _start_all(copies):
    for idx, cp in enumerate(copies):
        cp.start(priority=idx % 2)


def _decode_fetch(r, group, first, last):
    slot = (group - first) % 2

    @pl.when(group == first)
    def _():
        _start_all(_page_copies(r, first, 0))

    @pl.when(group < last)
    def _():
        _start_all(_page_copies(r, group + 1, 1 - slot))

    for cp in _page_copies(r, None, slot):
        cp.wait()

    n_groups = r.pt.shape[1] // PAGES_PER_STEP
    seq_i = group // n_groups

    @pl.when(group % n_groups == 0)
    def _():
        qf = _head_rows(r.q[seq_i], N_FOX_HEADS, FOX_WIDTH)
        s_new = jnp.sum(qf * r.knew[seq_i], axis=-1, keepdims=True)
        r.m[...] = jnp.broadcast_to(s_new, r.m.shape)
        r.l[...] = jnp.ones_like(r.l)
        r.cn[...] = jnp.ones_like(r.cn)
        r.carry[...] = jnp.broadcast_to(r.lfnew[seq_i], r.carry.shape)
        r.acc[...] = jnp.zeros_like(r.acc)


def _decode_compute(r, group, first):
    n = PAGES_PER_STEP
    n_groups = r.pt.shape[1] // n
    slot = (group - first) % 2
    qb = _head_rows(r.q[group // n_groups], N_FOX_HEADS, FOX_WIDTH).astype(BF16)
    x = r.lfbuf[slot].reshape(n * N_FOX_HEADS, PAGE_SIZE)
    wt = _dot3(x, r.tw[...])
    later = r.carry[...]
    scores = [None] * n
    for j in reversed(range(n)):
        rows = slice(j * N_FOX_HEADS, (j + 1) * N_FOX_HEADS)
        bias = wt[rows, :PAGE_SIZE] + later
        later = later + wt[rows, PAGE_SIZE:]
        scores[j] = jnp.dot(qb, r.kbuf[slot, j].astype(BF16), preferred_element_type=F32) + bias
    r.carry[...] = later
    z = jnp.concatenate(scores, axis=-1)
    m_old = r.m[...]
    m_new = jnp.maximum(m_old, jnp.max(z, axis=-1, keepdims=True))
    alpha = jnp.exp(m_old - m_new)
    p = jnp.exp(z - m_new[:, 0:1])
    r.l[...] = alpha * r.l[...] + jnp.sum(p, axis=-1, keepdims=True)
    r.cn[...] = alpha * r.cn[...]
    r.m[...] = m_new
    for h in range(N_FOX_HEADS):
        a = r.acc[h] * alpha[h:h + 1, :]
        for j in range(n):
            a = a + r.vbuf[slot, j, h] * p[h:h + 1, j * PAGE_SIZE:(j + 1) * PAGE_SIZE]
        r.acc[h] = a


def _decode_finish(r, group):
    n_groups = r.pt.shape[1] // PAGES_PER_STEP
    seq_i = group // n_groups

    @pl.when(group % n_groups == n_groups - 1)
    def _():
        acc2d = r.acc[...].reshape(FOX_WIDTH, PAGE_SIZE)
        ones = jnp.ones((BF16_SUBLANES, PAGE_SIZE), BF16)
        past = _dot3_nt(ones, acc2d)[0:1]
        new = _heads_to_row(r.cn[...], FOX_WIDTH) * r.vnew[seq_i]
        r.orow[...] = (past + new) / _heads_to_row(r.l[...], FOX_WIDTH)
        cp = pltpu.make_async_copy(r.orow, r.o_hbm.at[seq_i], r.osem)
        cp.start()
        cp.wait()


def _decode_operands(page_table, q, knew, vnew, lfnew, lf_cache, k_cache, v_cache, o_prev):
    del page_table
    n = PAGES_PER_STEP
    pos = jnp.arange(PAGE_SIZE)
    tw = jnp.concatenate([_ones_where(pos[:, None] > pos[None, :]),
                          jnp.ones((PAGE_SIZE, PAGE_SIZE), BF16)], axis=1)
    hbm = pl.BlockSpec(memory_space=pl.ANY)
    small = (q, knew, vnew, lfnew, tw)
    args = small + (lf_cache, k_cache, v_cache, o_prev)
    in_specs = [_resident(a.shape) for a in small] + [hbm] * 4
    scratch = ([pltpu.VMEM((2, n, N_FOX_HEADS, PAGE_SIZE), F32),
                pltpu.VMEM((2, n, FOX_WIDTH, PAGE_SIZE), F32),
                pltpu.VMEM((2, n, N_FOX_HEADS, HEAD_DIM, PAGE_SIZE), F32),
                pltpu.SemaphoreType.DMA((2, 3, n)),
                pltpu.VMEM((1, FOX_WIDTH), F32),
                pltpu.SemaphoreType.DMA(())]
               + [pltpu.VMEM((N_FOX_HEADS, LANES), F32)] * 4
               + [pltpu.VMEM((N_FOX_HEADS, HEAD_DIM, PAGE_SIZE), F32)])
    assert len(args) == N_DECODE_INPUTS and len(scratch) == N_DECODE_SCRATCH
    return args, in_specs, hbm, jax.ShapeDtypeStruct(o_prev.shape, o_prev.dtype), scratch


def _mem_decode_kernel(q_ref, mkt_ref, mvt_ref, o_ref):
    for b in range(q_ref.shape[0]):
        qb = _head_rows(q_ref[b], SUBLANES, MEM_WIDTH).astype(BF16)
        s = jnp.dot(qb, mkt_ref[b].astype(BF16), preferred_element_type=F32)
        p = jnp.exp(s - jnp.max(s, axis=-1, keepdims=True))
        l = jnp.sum(p, axis=-1, keepdims=True)
        o = lax.dot_general(p.astype(BF16), mvt_ref[b].astype(BF16), _NT, preferred_element_type=F32) / l
        o = jnp.where(_head_mask(SUBLANES, MEM_WIDTH), o, 0.0)
        o_ref[b] = jnp.sum(o, axis=0, keepdims=True).astype(o_ref.dtype)


def _mem_decode(qm, mkt, mvt):
    bsz = qm.shape[0]
    per_step = MEM_DECODE_SEQS
    return pl.pallas_call(
        _mem_decode_kernel,
        grid=(bsz // per_step,),
        in_specs=[pl.BlockSpec((per_step, 1, MEM_WIDTH), lambda b: (b, 0, 0)),
                  pl.BlockSpec((per_step, MEM_WIDTH, N_MEM), lambda b: (b, 0, 0)),
                  pl.BlockSpec((per_step, MEM_WIDTH, N_MEM), lambda b: (b, 0, 0))],
        out_specs=pl.BlockSpec((per_step, 1, MEM_WIDTH), lambda b: (b, 0, 0)),
        out_shape=jax.ShapeDtypeStruct((bsz, 1, MEM_WIDTH), BF16),
        compiler_params=_params(1),
        name="mem_decode",
    )(qm, mkt, mvt)


def _conv_norm_act(conv, gcn_ref, bcn_ref):
    mu = jnp.mean(conv, axis=-1, keepdims=True)
    d = conv - mu
    var = jnp.mean(d * d, axis=-1, keepdims=True)
    y = d * lax.rsqrt(var + EPS) * gcn_ref[...] + bcn_ref[...]
    return y * _sigmoid(y)


def _merge_project(x, yfox, yb, yc, g_ref, wa_ref, wb_ref, wc_ref, wo_ref, gffn_ref, x1_ref, h2_ref):
    ga = g_ref[:, :D_MODEL].astype(F32)
    gb = g_ref[:, D_MODEL:2 * D_MODEL].astype(F32)
    gc = g_ref[:, 2 * D_MODEL:].astype(F32)
    merged = (ga * jnp.dot(yfox, wa_ref[...], preferred_element_type=F32)
              + gb * jnp.dot(yb.astype(BF16), wb_ref[...], preferred_element_type=F32)
              + gc * jnp.dot(yc, wc_ref[...], preferred_element_type=F32))
    x1 = x + jnp.dot(merged.astype(BF16), wo_ref[...], preferred_element_type=F32)
    x1_ref[...] = x1
    h2_ref[...] = (_rms_scale(x1) * gffn_ref[...]).astype(BF16)


def _tail_prompt_kernel(x_ref, yfox_ref, u_ref, uhalo_ref, qm_ref, g_ref, mkt_ref, mv_ref,
                        wdw_ref, gcn_ref, bcn_ref, wa_ref, wb_ref, wc_ref, wo_ref, gffn_ref,
                        x1_ref, h2_ref, ubuf_ref, ushift_ref, *, tiles_per_seq):
    i = pl.program_id(0)
    tm = x_ref.shape[0]
    halo = uhalo_ref[...]
    ubuf_ref[0:CONV_HALO, :] = jnp.where(i % tiles_per_seq == 0, jnp.zeros_like(halo), halo)
    ubuf_ref[CONV_HALO:, :] = u_ref[...]
    base = CONV_HALO - (CONV_WIDTH - 1)
    reach = CONV_HALO - SUBLANES + tm
    for phase in range(1, SUBLANES):
        ushift_ref[phase, 0:reach, :] = ubuf_ref[phase:phase + reach, :]
    chunk = 64
    convs = []
    for r0 in range(0, tm, chunk):
        acc = jnp.zeros((chunk, CONV_CH), F32)
        for j in range(CONV_WIDTH):
            phase, lo = (base + j) % SUBLANES, (base + j) // SUBLANES * SUBLANES + r0
            rows = ubuf_ref[lo:lo + chunk, :] if phase == 0 else ushift_ref[phase, lo:lo + chunk, :]
            acc = acc + wdw_ref[j:j + 1, :] * rows
        convs.append(acc)
    yb = _conv_norm_act(jnp.concatenate(convs, axis=0), gcn_ref, bcn_ref)

    qm = qm_ref[...]
    lane = lax.broadcasted_iota(jnp.int32, (tm, MEM_WIDTH), 1)
    mkt = mkt_ref[0]
    mv = mv_ref[...]
    yc = jnp.zeros((tm, MEM_WIDTH), F32)
    for h in range(N_MEM_HEADS):
        in_head = (lane >= h * HEAD_DIM) & (lane < (h + 1) * HEAD_DIM)
        qh = jnp.where(in_head, qm, jnp.zeros((), qm.dtype))
        s = jnp.dot(qh, mkt, preferred_element_type=F32)
        p = jnp.exp(s - jnp.max(s, axis=-1, keepdims=True))
        o = jnp.dot(p.astype(BF16), mv, preferred_element_type=F32) / jnp.sum(p, axis=-1, keepdims=True)
        yc = jnp.where(in_head, o, yc)

    _merge_project(x_ref[...], yfox_ref[...], yb, yc.astype(BF16), g_ref,
                   wa_ref, wb_ref, wc_ref, wo_ref, gffn_ref, x1_ref, h2_ref)


def _tail_sample_kernel(x_ref, yfox_ref, u_ref, state_ref, yc_ref, g_ref,
                        wdw_ref, gcn_ref, bcn_ref, wa_ref, wb_ref, wc_ref, wo_ref, gffn_ref,
                        x1_ref, h2_ref):
    conv = wdw_ref[CONV_WIDTH - 1:CONV_WIDTH, :] * u_ref[...]
    for j in range(CONV_WIDTH - 1):
        conv = conv + wdw_ref[j:j + 1, :] * state_ref[j]
    yb = _conv_norm_act(conv, gcn_ref, bcn_ref)
    _merge_project(x_ref[...], yfox_ref[...], yb, yc_ref[...], g_ref,
                   wa_ref, wb_ref, wc_ref, wo_ref, gffn_ref, x1_ref, h2_ref)


def _tail_prompt(x2d, yfox, u, qm, g, mktb, mvb, tiles_per_seq, tm, *ws):
    n = x2d.shape[0]
    rows = lambda w: pl.BlockSpec((tm, w), lambda i: (i, 0))
    halo_blocks = tm // CONV_HALO
    return pl.pallas_call(
        functools.partial(_tail_prompt_kernel, tiles_per_seq=tiles_per_seq),
        grid=(n // tm,),
        in_specs=[rows(D_MODEL), rows(FOX_WIDTH), rows(CONV_CH),
                  pl.BlockSpec((CONV_HALO, CONV_CH), lambda i: (jnp.maximum(i * halo_blocks - 1, 0), 0)),
                  rows(MEM_WIDTH), rows(3 * D_MODEL),
                  pl.BlockSpec((1, MEM_WIDTH, N_MEM), lambda i: (i // tiles_per_seq, 0, 0)),
                  pl.BlockSpec((N_MEM, MEM_WIDTH), lambda i: (i // tiles_per_seq, 0))]
                 + [_resident(w.shape) for w in ws],
        out_specs=[rows(D_MODEL), rows(D_MODEL)],
        out_shape=[jax.ShapeDtypeStruct((n, D_MODEL), F32), jax.ShapeDtypeStruct((n, D_MODEL), BF16)],
        scratch_shapes=[pltpu.VMEM((CONV_HALO + tm, CONV_CH), F32),
                        pltpu.VMEM((SUBLANES, CONV_HALO + tm, CONV_CH), F32)],
        compiler_params=_params(1),
        name="tail_prompt",
    )(x2d, yfox, u, u, qm, g, mktb, mvb, *ws)


def _tail_sample(x2d, yfox, u, state_t, yc, g, *ws):
    n = x2d.shape[0]
    full = lambda a: pl.BlockSpec(a.shape, lambda i: (0,) * a.ndim)
    ins = (x2d, yfox, u, state_t, yc, g)
    return pl.pallas_call(
        _tail_sample_kernel,
        grid=(1,),
        in_specs=[full(a) for a in ins] + [_resident(w.shape) for w in ws],
        out_specs=[pl.BlockSpec((n, D_MODEL), lambda i: (0, 0))] * 2,
        out_shape=[jax.ShapeDtypeStruct((n, D_MODEL), F32), jax.ShapeDtypeStruct((n, D_MODEL), BF16)],
        compiler_params=_params(1),
        name="tail_sample",
    )(*ins, *ws)


def _ffn_finish(x1_ref, acc_ref, gfin_ref, y_ref):
    x2 = x1_ref[...] + acc_ref[...]
    y_ref[...] = _rms_scale(x2) * gfin_ref[...]


def _ffn_prompt_kernel(pt_ref, h2_ref, halo_ref, x1_ref, wg_ref, wu_ref, wd_ref, wcv_ref, gfin_ref, *refs,
                       tiles_per_seq, first, last):
    y_ref, gtail_ref, o_hbm, hcat_ref, gbuf_ref, acc_ref = refs[N_DECODE_INPUTS:N_DECODE_INPUTS + 6]
    dec = _decode_refs(pt_ref, refs[:N_DECODE_INPUTS], o_hbm, refs[N_DECODE_INPUTS + 6:])
    n_chunks = wg_ref.shape[0]
    hosted_per_tile = n_chunks // FFN_CHUNKS_PER_GROUP
    i = pl.program_id(0)
    tm = h2_ref.shape[0]
    halo = halo_ref[...]
    hcat_ref[0:FFN_HALO, :] = jnp.where(i % tiles_per_seq == 0, jnp.zeros_like(halo), halo)
    hcat_ref[FFN_HALO:, :] = h2_ref[...]
    acc_ref[...] = jnp.zeros_like(acc_ref)

    def chunk(c, group):
        if group is not None:
            _decode_fetch(dec, group, first, last)
            _decode_compute(dec, group, first)
        gbuf_ref[...] = jnp.dot(hcat_ref[...], wg_ref[c], preferred_element_type=F32)
        up = jnp.dot(h2_ref[...], wu_ref[c], preferred_element_type=F32)
        w = wcv_ref[c]
        cv = (w[0:1, :] * gbuf_ref[FFN_HALO - 2:FFN_HALO - 2 + tm, :]
              + w[1:2, :] * gbuf_ref[FFN_HALO - 1:FFN_HALO - 1 + tm, :]
              + w[2:3, :] * gbuf_ref[FFN_HALO:FFN_HALO + tm, :])
        act = (cv * _sigmoid(cv) * up).astype(BF16)
        acc_ref[...] += jnp.dot(act, wd_ref[c], preferred_element_type=F32)
        gtail_ref[0, c] = gbuf_ref[FFN_HALO + tm - SUBLANES:FFN_HALO + tm, :]
        if group is not None:
            _decode_finish(dec, group)

    def plain_chunk(c, carry):
        chunk(c, None)
        return carry

    for k in range(hosted_per_tile):
        c0 = k * FFN_CHUNKS_PER_GROUP
        chunk(c0, first + i * hosted_per_tile + k)
        lax.fori_loop(c0 + 1, c0 + FFN_CHUNKS_PER_GROUP, plain_chunk, 0)
    _ffn_finish(x1_ref, acc_ref, gfin_ref, y_ref)


def _ffn_sample_kernel(h2_ref, state_ref, x1_ref, wg_ref, wu_ref, wd_ref, wcv_ref, gfin_ref,
                       y_ref, gt_ref, acc_ref):
    acc_ref[...] = jnp.zeros_like(acc_ref)

    def chunk(c, carry):
        h2 = h2_ref[...]
        gt = jnp.dot(h2, wg_ref[c], preferred_element_type=F32)
        up = jnp.dot(h2, wu_ref[c], preferred_element_type=F32)
        w = wcv_ref[c]
        cv = w[0:1, :] * state_ref[c, 0] + w[1:2, :] * state_ref[c, 1] + w[2:3, :] * gt
        act = (cv * _sigmoid(cv) * up).astype(BF16)
        acc_ref[...] += jnp.dot(act, wd_ref[c], preferred_element_type=F32)
        gt_ref[c] = gt
        return carry

    lax.fori_loop(0, wg_ref.shape[0], chunk, 0)
    _ffn_finish(x1_ref, acc_ref, gfin_ref, y_ref)


def _ffn_prompt_steps(n_rows, tm):
    assert (D_FF // FF_CHUNK) % FFN_CHUNKS_PER_GROUP == 0
    return (n_rows // tm) * (D_FF // FF_CHUNK // FFN_CHUNKS_PER_GROUP)


def _ffn_prompt(h2, x1, tiles_per_seq, tm, wg, wu, wd, wcv, gfin, page_table, decode_in, first):
    n = h2.shape[0]
    nt = n // tm
    n_chunks = wg.shape[0]
    last = first + _ffn_prompt_steps(n, tm) - 1
    d_args, d_specs, d_out_spec, d_out_shape, d_scratch = _decode_operands(page_table, *decode_in)
    rows = lambda w: pl.BlockSpec((tm, w), lambda i, pt: (i, 0))
    halo_blocks = tm // FFN_HALO
    grid_spec = pltpu.PrefetchScalarGridSpec(
        num_scalar_prefetch=1,
        grid=(nt,),
        in_specs=[rows(D_MODEL),
                  pl.BlockSpec((FFN_HALO, D_MODEL), lambda i, pt: (jnp.maximum(i * halo_blocks - 1, 0), 0)),
                  rows(D_MODEL), _resident(wg.shape), _resident(wu.shape), _resident(wd.shape),
                  _resident(wcv.shape), _resident(gfin.shape)] + d_specs,
        out_specs=[rows(D_MODEL), pl.BlockSpec((1, n_chunks, SUBLANES, FF_CHUNK), lambda i, pt: (i, 0, 0, 0)),
                   d_out_spec],
        scratch_shapes=[pltpu.VMEM((FFN_HALO + tm, D_MODEL), BF16),
                        pltpu.VMEM((FFN_HALO + tm, FF_CHUNK), F32),
                        pltpu.VMEM((tm, D_MODEL), F32)] + d_scratch,
    )
    return pl.pallas_call(
        functools.partial(_ffn_prompt_kernel, tiles_per_seq=tiles_per_seq, first=first, last=last),
        grid_spec=grid_spec,
        out_shape=[jax.ShapeDtypeStruct((n, D_MODEL), F32),
                   jax.ShapeDtypeStruct((nt, n_chunks, SUBLANES, FF_CHUNK), F32), d_out_shape],
        input_output_aliases={8 + len(d_args): 2},
        compiler_params=_params(1),
        name="ffn_prompt",
    )(page_table, h2, h2, x1, wg, wu, wd, wcv, gfin, *d_args)


def _ffn_sample(h2, state_c, x1, wg, wu, wd, wcv, gfin):
    n = h2.shape[0]
    n_chunks = wg.shape[0]
    full = lambda a: pl.BlockSpec(a.shape, lambda i: (0,) * a.ndim)
    return pl.pallas_call(
        _ffn_sample_kernel,
        grid=(1,),
        in_specs=[full(h2), full(state_c), full(x1), _resident(wg.shape), _resident(wu.shape),
                  _resident(wd.shape), _resident(wcv.shape), _resident(gfin.shape)],
        out_specs=[pl.BlockSpec((n, D_MODEL), lambda i: (0, 0)),
                   pl.BlockSpec((n_chunks, n, FF_CHUNK), lambda i: (0, 0, 0))],
        out_shape=[jax.ShapeDtypeStruct((n, D_MODEL), F32),
                   jax.ShapeDtypeStruct((n_chunks, n, FF_CHUNK), F32)],
        scratch_shapes=[pltpu.VMEM((n, D_MODEL), F32)],
        compiler_params=_params(1),
        name="ffn_sample",
    )(h2, state_c, x1, wg, wu, wd, wcv, gfin)


def _pad_rows(a, rows):
    return jnp.concatenate([a, jnp.zeros((rows - a.shape[0],) + a.shape[1:], a.dtype)], axis=0)


def _heads_last(xt, batch, n_heads, n_pos):
    return xt.reshape(batch, n_heads, HEAD_DIM, n_pos).transpose(0, 3, 1, 2)


def kernel(x_prompt, x_sample, cache_fox_k, cache_fox_v, cache_fox_logf, state_conv, state_ffn_conv, cache_mem_k, cache_mem_v, page_table, mem_prompt, g_mix, w_in, b_f, w_conv_dw, g_conv_norm, b_conv_norm, g_mem, w_mem_kv, w_a, w_b, w_c, w_o, g_ffn, w_ffn_in, w_ffn_conv, w_down, g_final):
    depth = g_mix.shape[0]
    assert depth == 1, "kernel is written for the single-layer trunk the problem states"
    bp, seq, _ = x_prompt.shape
    bs, dec_seq, _ = x_sample.shape
    assert dec_seq == 1 and seq % FFN_ROW_TILE == 0 and seq % ATTN_BLOCK == 0 and bs % LANES == 0
    n_pool = cache_fox_k.shape[1]
    n_pages = page_table.shape[1]
    assert n_pages % PAGES_PER_STEP == 0 and cache_fox_k.shape[2] == PAGE_SIZE
    l = 0

    w = w_in[l]
    wt = w.T
    o_f = 3 * FOX_WIDTH
    o_glu = o_f + N_FOX_HEADS
    o_qm = o_glu + 2 * CONV_CH
    o_gate = o_qm + MEM_WIDTH
    front_w = (
        g_mix[l][None, :],
        w[:, :FOX_WIDTH].astype(BF16),
        wt[FOX_WIDTH:o_f].astype(BF16),
        w[:, 2 * FOX_WIDTH:o_f].astype(BF16),
        _pad_rows(wt[o_f:o_glu], BF16_SUBLANES).astype(BF16),
        b_f[l][:, None],
        w[:, o_glu:o_qm].astype(BF16),
        w[:, o_qm:o_gate].astype(BF16),
        w[:, o_gate:].astype(BF16),
    )
    tail_w = (_pad_rows(w_conv_dw[l], CONV_HALO), g_conv_norm[l][None, :], b_conv_norm[l][None, :],
              w_a[l].astype(BF16), w_b[l].astype(BF16), w_c[l].astype(BF16), w_o[l].astype(BF16),
              g_ffn[l][None, :])
    n_chunks = D_FF // FF_CHUNK
    wfi = w_ffn_in[l]
    wg = wfi[:, :D_FF].reshape(D_MODEL, n_chunks, FF_CHUNK).transpose(1, 0, 2).astype(BF16)
    wu = wfi[:, D_FF:].reshape(D_MODEL, n_chunks, FF_CHUNK).transpose(1, 0, 2).astype(BF16)
    wd = w_down[l].reshape(n_chunks, FF_CHUNK, D_MODEL).astype(BF16)
    wcv = _pad_rows(w_ffn_conv[l], SUBLANES).reshape(SUBLANES, n_chunks, FF_CHUNK).transpose(1, 0, 2)
    ffn_w = (wg, wu, wd, wcv, g_final[None, :])

    xs2 = x_sample.reshape(bs, D_MODEL)
    (q_s, kt_s, _, vt_s, _, lft_s, _, u_s, qm_s, g_s, krow_s, vrow_s) = _front(xs2, 1, bs, True, *front_w)
    row3 = lambda a: a.reshape(bs, 1, a.shape[-1])
    decode_in = [row3(q_s), row3(krow_s), row3(vrow_s), lft_s[0].T[:, :, None],
                 cache_fox_logf[l].transpose(0, 2, 1),
                 cache_fox_k[l].transpose(0, 2, 3, 1).reshape(n_pool, FOX_WIDTH, PAGE_SIZE),
                 cache_fox_v[l].transpose(0, 2, 3, 1),
                 jnp.zeros((bs, 1, FOX_WIDTH), F32)]
    groups_per_seq = n_pages // PAGES_PER_STEP
    in_fox = _fox_prompt_steps(bp, seq)
    in_ffn = _ffn_prompt_steps(bp * seq, FFN_ROW_TILE)
    assert in_fox % groups_per_seq == 0 and in_fox + in_ffn == bs * groups_per_seq, \
        "the prompt kernels' steps must cover the decode page groups exactly"

    xp2 = x_prompt.reshape(bp * seq, D_MODEL)
    q_p, kt_p, ktb_p, vt_p, vb_p, lft_p, ct_p, u_p, qm_p, g_p = _front(xp2, bp, FRONT_ROW_TILE, False, *front_w)
    wmt = w_mem_kv[l].T.astype(BF16)
    mkt_p, mvt_p, mktb_p, mvb_p = _memkv(mem_prompt.reshape(bp * N_MEM, D_MODEL), bp, g_mem[l][None, :],
                                         wmt, w_mem_kv[l][:, MEM_WIDTH:].astype(BF16))
    yfox_p, decode_in[-1] = _fox_prompt(q_p, ktb_p, vb_p, ct_p.reshape(bp, FOX_WIDTH // LANES, 2, seq), bp, seq,
                                        page_table, decode_in, 0)
    x1_p, h2_p = _tail_prompt(xp2, yfox_p, u_p, qm_p, g_p, mktb_p, mvb_p, seq // ROW_TILE, ROW_TILE, *tail_w)
    ffn_tiles = seq // FFN_ROW_TILE
    y_p, gtail_p, yfox_s = _ffn_prompt(h2_p, x1_p, ffn_tiles, FFN_ROW_TILE, *ffn_w,
                                       page_table, decode_in, in_fox)

    yc_s = _mem_decode(row3(qm_s),
                       cache_mem_k[l].transpose(0, 2, 3, 1).reshape(bs, MEM_WIDTH, N_MEM),
                       cache_mem_v[l].transpose(0, 2, 3, 1).reshape(bs, MEM_WIDTH, N_MEM))
    x1_s, h2_s = _tail_sample(xs2, yfox_s.reshape(bs, FOX_WIDTH).astype(BF16), u_s,
                              state_conv[l].transpose(1, 0, 2), yc_s.reshape(bs, MEM_WIDTH), g_s, *tail_w)
    ffn_state_c = state_ffn_conv[l].reshape(bs, FFN_CONV_WIDTH - 1, n_chunks, FF_CHUNK).transpose(2, 1, 0, 3)
    y_s, gt_s = _ffn_sample(h2_s, ffn_state_c, x1_s, *ffn_w)

    y_prompt = y_p.reshape(bp, seq, D_MODEL)
    y_sample = y_s.reshape(bs, 1, D_MODEL)
    fox_k_p = _heads_last(kt_p, bp, N_FOX_HEADS, seq)[None]
    fox_v_p = _heads_last(vt_p, bp, N_FOX_HEADS, seq)[None]
    fox_logf_p = lft_p.transpose(0, 2, 1)[None]
    conv_state_p = u_p.reshape(bp, seq, CONV_CH)[:, seq - (CONV_WIDTH - 1):][None]
    gt_last = gtail_p.reshape(bp, ffn_tiles, n_chunks, SUBLANES, FF_CHUNK)[
        :, -1, :, SUBLANES - (FFN_CONV_WIDTH - 1):, :]
    ffn_state_p = gt_last.transpose(0, 2, 1, 3).reshape(1, bp, FFN_CONV_WIDTH - 1, D_FF)
    mem_k_p = _heads_last(mkt_p, bp, N_MEM_HEADS, N_MEM)[None]
    mem_v_p = _heads_last(mvt_p, bp, N_MEM_HEADS, N_MEM)[None]
    fox_k_s = _heads_last(kt_s, 1, N_FOX_HEADS, bs).reshape(1, bs, 1, N_FOX_HEADS, HEAD_DIM)
    fox_v_s = _heads_last(vt_s, 1, N_FOX_HEADS, bs).reshape(1, bs, 1, N_FOX_HEADS, HEAD_DIM)
    fox_logf_s = lft_s[0].T.reshape(1, bs, 1, N_FOX_HEADS)
    conv_state_s = jnp.concatenate([state_conv[l][:, 1:], u_s[:, None, :]], axis=1)[None]
    gt_new = gt_s.transpose(1, 0, 2).reshape(bs, 1, D_FF)
    ffn_state_s = jnp.concatenate([state_ffn_conv[l][:, 1:], gt_new], axis=1)[None]
    return (y_prompt, y_sample, fox_k_p, fox_v_p, fox_logf_p, conv_state_p, ffn_state_p, mem_k_p, mem_v_p,
            fox_k_s, fox_v_s, fox_logf_s, conv_state_s, ffn_state_s)
```

```python
import functools
from typing import Any, NamedTuple

import jax
import jax.numpy as jnp
from jax import lax
from jax.experimental import pallas as pl
from jax.experimental.pallas import tpu as pltpu

F32 = jnp.float32
BF16 = jnp.bfloat16

D_MODEL = 1024
HEAD_DIM = 64
N_FOX_HEADS = 8
FOX_WIDTH = N_FOX_HEADS * HEAD_DIM
CONV_CH = 256
CONV_WIDTH = 31
N_MEM = 256
N_MEM_HEADS = 4
MEM_WIDTH = N_MEM_HEADS * HEAD_DIM
D_FF = 3 * D_MODEL
FFN_CONV_WIDTH = 3
PAGE_SIZE = 128
EPS = 1e-6
ATTN_SCALE = HEAD_DIM ** -0.5
NEG_INF = -1e30

LANES = 128
SUBLANES = 8
BF16_SUBLANES = 16
FRONT_ROW_TILE = 512
ROW_TILE = 256
FFN_ROW_TILE = 512
ATTN_BLOCK = 512
CONV_HALO = 32
FFN_HALO = BF16_SUBLANES
FF_CHUNK = 512
PAGES_PER_STEP = 16
FFN_CHUNKS_PER_GROUP = 3
MEM_DECODE_SEQS = 16
VMEM_LIMIT_BYTES = 56 * 1024 * 1024

_NT = (((1,), (1,)), ((), ()))


def _params(n_axes):
    return pltpu.CompilerParams(dimension_semantics=("arbitrary",) * n_axes,
                                vmem_limit_bytes=VMEM_LIMIT_BYTES)


def _resident(shape):
    nd = len(shape)
    return pl.BlockSpec(shape, lambda *_: (0,) * nd, pipeline_mode=pl.Buffered(1))


def _rms_scale(x):
    return x * lax.rsqrt(jnp.mean(x * x, axis=-1, keepdims=True) + EPS)


def _sigmoid(x):
    return 1.0 / (1.0 + jnp.exp(-x))


def _log_sigmoid(x):
    return jnp.minimum(x, 0.0) - jnp.log(1.0 + jnp.exp(-jnp.abs(x)))


def _split3(x):
    hi = x.astype(BF16)
    r1 = x - hi.astype(F32)
    mid = r1.astype(BF16)
    lo = (r1 - mid.astype(F32)).astype(BF16)
    return hi, mid, lo


def _dot3(a, b01):
    return sum(jnp.dot(t, b01, preferred_element_type=F32) for t in _split3(a))


def _dot3_left(a01, b):
    return sum(jnp.dot(a01, t, preferred_element_type=F32) for t in _split3(b))


def _dot3_nt(a01, b):
    return sum(lax.dot_general(a01, t, _NT, preferred_element_type=F32) for t in _split3(b))


def _ones_where(cond):
    return jnp.where(cond, 1.0, 0.0).astype(BF16)


def _head_mask(n_rows, width):
    r = lax.broadcasted_iota(jnp.int32, (n_rows, width), 0)
    c = lax.broadcasted_iota(jnp.int32, (n_rows, width), 1)
    return (c >= r * HEAD_DIM) & (c < (r + 1) * HEAD_DIM)


def _head_rows(row, n_rows, width):
    return jnp.where(_head_mask(n_rows, width), jnp.broadcast_to(row.astype(F32), (n_rows, width)), 0.0)


def _heads_to_row(col, width):
    n_rows = col.shape[0]
    spread = jnp.where(_head_mask(n_rows, width), jnp.broadcast_to(col[:, 0:1], (n_rows, width)), 0.0)
    return jnp.sum(spread, axis=0, keepdims=True)


def _front_kernel(x_ref, gmix_ref, wq_ref, wkvt_ref, wft_ref, bf_ref, wglu_ref, wqm_ref, wgate_ref,
                  q_ref, kt_ref, ktb_ref, vt_ref, vb_ref, lft_ref, ct_ref, u_ref, qm_ref, g_ref, *rest,
                  tiles_per_seq, emit_rows):
    if emit_rows:
        krow_ref, vrow_ref, carry_ref = rest
    else:
        (carry_ref,) = rest
    i = pl.program_id(0)
    tm = x_ref.shape[0]
    h = (_rms_scale(x_ref[...]) * gmix_ref[...]).astype(BF16)

    q_ref[...] = (jnp.dot(h, wq_ref[...], preferred_element_type=F32) * ATTN_SCALE).astype(BF16)
    kvt = lax.dot_general(wkvt_ref[...], h, _NT, preferred_element_type=F32)
    kt = kvt[:FOX_WIDTH]
    kt_ref[0] = kt
    ktb_ref[0] = kt.astype(BF16)
    vt_ref[0] = kvt[FOX_WIDTH:]
    v = kvt[FOX_WIDTH:].T
    vb_ref[...] = v.astype(BF16)
    if emit_rows:
        vrow_ref[...] = v
        krow_ref[...] = lax.dot_general(h, wkvt_ref[0:FOX_WIDTH, :], _NT, preferred_element_type=F32)

    flt = lax.dot_general(wft_ref[...], h, _NT, preferred_element_type=F32)[:N_FOX_HEADS] + bf_ref[...]
    lft = _log_sigmoid(flt)
    lft_ref[0] = lft

    @pl.when(i % tiles_per_seq == 0)
    def _():
        carry_ref[...] = jnp.zeros_like(carry_ref)

    r = lax.broadcasted_iota(jnp.int32, (tm, tm), 0)
    c = lax.broadcasted_iota(jnp.int32, (tm, tm), 1)
    csum = _dot3(lft, _ones_where(r <= c)) + carry_ref[:, 0:1]
    ct_ref[0] = csum
    carry_ref[...] = jnp.broadcast_to(csum[:, tm - 1:tm], carry_ref.shape)

    glu = jnp.dot(h, wglu_ref[...], preferred_element_type=F32)
    u_ref[...] = glu[:, :CONV_CH] * _sigmoid(glu[:, CONV_CH:])
    qm = jnp.dot(h, wqm_ref[...], preferred_element_type=F32)
    qm_ref[...] = (qm * ATTN_SCALE).astype(BF16)
    g_ref[...] = _sigmoid(jnp.dot(h, wgate_ref[...], preferred_element_type=F32)).astype(BF16)


def _front(x2d, batch, tm, emit_rows, *weights):
    n = x2d.shape[0]
    seq = n // batch
    tiles_per_seq = seq // tm
    rows = lambda w: pl.BlockSpec((tm, w), lambda i: (i, 0))
    cols = lambda r: pl.BlockSpec((1, r, tm), lambda i: (i // tiles_per_seq, 0, i % tiles_per_seq))
    out = [
        (jax.ShapeDtypeStruct((n, FOX_WIDTH), BF16), rows(FOX_WIDTH)),
        (jax.ShapeDtypeStruct((batch, FOX_WIDTH, seq), F32), cols(FOX_WIDTH)),
        (jax.ShapeDtypeStruct((batch, FOX_WIDTH, seq), BF16), cols(FOX_WIDTH)),
        (jax.ShapeDtypeStruct((batch, FOX_WIDTH, seq), F32), cols(FOX_WIDTH)),
        (jax.ShapeDtypeStruct((n, FOX_WIDTH), BF16), rows(FOX_WIDTH)),
        (jax.ShapeDtypeStruct((batch, N_FOX_HEADS, seq), F32), cols(N_FOX_HEADS)),
        (jax.ShapeDtypeStruct((batch, N_FOX_HEADS, seq), F32), cols(N_FOX_HEADS)),
        (jax.ShapeDtypeStruct((n, CONV_CH), F32), rows(CONV_CH)),
        (jax.ShapeDtypeStruct((n, MEM_WIDTH), BF16), rows(MEM_WIDTH)),
        (jax.ShapeDtypeStruct((n, 3 * D_MODEL), BF16), rows(3 * D_MODEL)),
    ]
    if emit_rows:
        out += [(jax.ShapeDtypeStruct((n, FOX_WIDTH), F32), rows(FOX_WIDTH))] * 2
    return pl.pallas_call(
        functools.partial(_front_kernel, tiles_per_seq=tiles_per_seq, emit_rows=emit_rows),
        grid=(n // tm,),
        in_specs=[rows(D_MODEL)] + [_resident(w.shape) for w in weights],
        out_specs=[s for _, s in out],
        out_shape=[s for s, _ in out],
        scratch_shapes=[pltpu.VMEM((N_FOX_HEADS, LANES), F32)],
        compiler_params=_params(1),
        name="front_rows" if emit_rows else "front",
    )(x2d, *weights)


def _memkv_kernel(m_ref, g_ref, wt_ref, wv_ref, mkt_ref, mvt_ref, mktb_ref, mvb_ref):
    h = (_rms_scale(m_ref[...]) * g_ref[...]).astype(BF16)
    kvt = lax.dot_general(wt_ref[...], h, _NT, preferred_element_type=F32)
    mkt_ref[0] = kvt[:MEM_WIDTH]
    mvt_ref[0] = kvt[MEM_WIDTH:]
    mktb_ref[0] = kvt[:MEM_WIDTH].astype(BF16)
    mvb_ref[...] = jnp.dot(h, wv_ref[...], preferred_element_type=F32).astype(BF16)


def _memkv(mem2d, batch, gmem, wkvt, wv):
    rows = lambda w: pl.BlockSpec((N_MEM, w), lambda i: (i, 0))
    per_b = pl.BlockSpec((1, MEM_WIDTH, N_MEM), lambda i: (i, 0, 0))
    t_shape = (batch, MEM_WIDTH, N_MEM)
    return pl.pallas_call(
        _memkv_kernel,
        grid=(batch,),
        in_specs=[rows(D_MODEL), _resident(gmem.shape), _resident(wkvt.shape), _resident(wv.shape)],
        out_specs=[per_b, per_b, per_b, rows(MEM_WIDTH)],
        out_shape=[jax.ShapeDtypeStruct(t_shape, F32), jax.ShapeDtypeStruct(t_shape, F32),
                   jax.ShapeDtypeStruct(t_shape, BF16), jax.ShapeDtypeStruct((batch * N_MEM, MEM_WIDTH), BF16)],
        compiler_params=_params(1),
        name="memkv",
    )(mem2d, gmem, wkvt, wv)


def _fox_prompt_kernel(pt_ref, q_ref, kt_ref, v_ref, ct_ref, *refs, first, last):
    o_ref = refs[N_DECODE_INPUTS]
    dec = _decode_refs(pt_ref, refs[:N_DECODE_INPUTS], refs[N_DECODE_INPUTS + 1], refs[N_DECODE_INPUTS + 2:])
    nq = pl.num_programs(2)
    i = pl.program_id(2)
    steps_before = (pl.program_id(0) * pl.num_programs(1) + pl.program_id(1)) * (nq * (nq - 1) // 2) \
        + i * (i - 1) // 2
    tq = q_ref.shape[0]
    tk = tq
    q = q_ref[...]
    lane = lax.broadcasted_iota(jnp.int32, (tq, LANES), 1)
    zero = jnp.zeros((), q.dtype)
    q2 = jnp.concatenate([jnp.where(lane < HEAD_DIM, q, zero), jnp.where(lane >= HEAD_DIM, q, zero)], axis=0)

    ones = jnp.ones((tk, LANES), BF16)

    def step(j, carry, masked):
        m, accl = carry
        if not masked:
            group = first + steps_before + j
            _decode_fetch(dec, group, first, last)
            _decode_compute(dec, group, first)
        start = pl.multiple_of(j * tk, tk)
        s = jnp.dot(q2, kt_ref[0, :, pl.ds(start, tk)], preferred_element_type=F32)
        ck = ct_ref[0, 0, :, pl.ds(start, tk)]
        z = jnp.concatenate([s[:tq] - ck[0:1], s[tq:] - ck[1:2]], axis=0)
        if masked:
            row = lax.broadcasted_iota(jnp.int32, (2 * tq, tk), 0)
            col = lax.broadcasted_iota(jnp.int32, (2 * tq, tk), 1)
            z = jnp.where(col <= (row & (tq - 1)), z, NEG_INF)
        m_new = jnp.maximum(m, jnp.max(z, axis=-1, keepdims=True))
        alpha = jnp.exp(m - m_new)
        p = jnp.exp((z - m_new).astype(BF16))
        v1 = jnp.concatenate([v_ref[pl.ds(start, tk), :], ones], axis=1)
        accl = alpha * accl + jnp.dot(p, v1, preferred_element_type=F32)
        if not masked:
            _decode_finish(dec, group)
        return m_new, accl

    init = (jnp.full((2 * tq, 1), NEG_INF, F32), jnp.zeros((2 * tq, 2 * LANES), F32))
    carry = lax.fori_loop(0, i, functools.partial(step, masked=False), init)
    _, accl = step(i, carry, True)
    o = accl[:, :LANES] / accl[:, LANES:]
    o_ref[...] = jnp.where(lane < HEAD_DIM, o[:tq], o[tq:]).astype(o_ref.dtype)


def _fox_prompt_steps(batch, seq):
    nq = seq // ATTN_BLOCK
    return batch * (FOX_WIDTH // LANES) * (nq * (nq - 1) // 2)


def _fox_prompt(q, ktb, vb, ct, batch, seq, page_table, decode_in, first):
    assert ATTN_BLOCK & (ATTN_BLOCK - 1) == 0
    nq = seq // ATTN_BLOCK
    n_pairs = FOX_WIDTH // LANES
    last = first + _fox_prompt_steps(batch, seq) - 1
    d_args, d_specs, d_out_spec, d_out_shape, d_scratch = _decode_operands(page_table, *decode_in)
    tile = pl.BlockSpec((ATTN_BLOCK, LANES), lambda b, p, i, pt: (b * nq + i, p))
    grid_spec = pltpu.PrefetchScalarGridSpec(
        num_scalar_prefetch=1,
        grid=(batch, n_pairs, nq),
        in_specs=[
            tile,
            pl.BlockSpec((1, LANES, seq), lambda b, p, i, pt: (b, p, 0)),
            pl.BlockSpec((seq, LANES), lambda b, p, i, pt: (b, p)),
            pl.BlockSpec((1, 1, 2, seq), lambda b, p, i, pt: (b, p, 0, 0)),
        ] + d_specs,
        out_specs=[tile, d_out_spec],
        scratch_shapes=d_scratch,
    )
    return pl.pallas_call(
        functools.partial(_fox_prompt_kernel, first=first, last=last),
        grid_spec=grid_spec,
        out_shape=[jax.ShapeDtypeStruct((batch * seq, FOX_WIDTH), BF16), d_out_shape],
        input_output_aliases={4 + len(d_args): 1},
        compiler_params=_params(3),
        name="fox_prompt",
    )(page_table, q, ktb, vb, ct, *d_args)


class _Decode(NamedTuple):
    pt: Any
    q: Any
    knew: Any
    vnew: Any
    lfnew: Any
    tw: Any
    lf_hbm: Any
    k_hbm: Any
    v_hbm: Any
    o_hbm: Any
    lfbuf: Any
    kbuf: Any
    vbuf: Any
    sem: Any
    orow: Any
    osem: Any
    m: Any
    l: Any
    cn: Any
    carry: Any
    acc: Any


N_DECODE_INPUTS = 9
N_DECODE_SCRATCH = 11


def _decode_refs(pt_ref, ins, o_hbm, scratch):
    assert len(ins) == N_DECODE_INPUTS and len(scratch) == N_DECODE_SCRATCH
    return _Decode(pt_ref, *ins[:N_DECODE_INPUTS - 1], o_hbm, *scratch)


def _page_copies(r, group, slot):
    n = PAGES_PER_STEP
    n_groups = r.pt.shape[1] // n
    if group is not None:
        seq_i = group // n_groups
        first = (n_groups - 1 - group % n_groups) * n
    copies = []
    for j in range(n):
        page = 0 if group is None else r.pt[seq_i, first + j]
        copies.append(pltpu.make_async_copy(r.lf_hbm.at[page], r.lfbuf.at[slot, j], r.sem.at[slot, 0, j]))
        copies.append(pltpu.make_async_copy(r.k_hbm.at[page], r.kbuf.at[slot, j], r.sem.at[slot, 1, j]))
        copies.append(pltpu.make_async_copy(r.v_hbm.at[page], r.vbuf.at[slot, j], r.sem.at[slot, 2, j]))
    return copies


def _start_all(copies):
    for idx, cp in enumerate(copies):
        cp.start(priority=idx % 2)


def _decode_fetch(r, group, first, last):
    slot = (group - first) % 2

    @pl.when(group == first)
    def _():
        _start_all(_page_copies(r, first, 0))

    @pl.when(group < last)
    def _():
        _start_all(_page_copies(r, group + 1, 1 - slot))

    for cp in _page_copies(r, None, slot):
        cp.wait()

    n_groups = r.pt.shape[1] // PAGES_PER_STEP
    seq_i = group // n_groups

    @pl.when(group % n_groups == 0)
    def _():
        qf = _head_rows(r.q[seq_i], N_FOX_HEADS, FOX_WIDTH)
        s_new = jnp.sum(qf * r.knew[seq_i], axis=-1, keepdims=True)
        r.m[...] = jnp.broadcast_to(s_new, r.m.shape)
        r.l[...] = jnp.ones_like(r.l)
        r.cn[...] = jnp.ones_like(r.cn)
        r.carry[...] = jnp.broadcast_to(r.lfnew[seq_i], r.carry.shape)
        r.acc[...] = jnp.zeros_like(r.acc)


def _decode_compute(r, group, first):
    n = PAGES_PER_STEP
    n_groups = r.pt.shape[1] // n
    slot = (group - first) % 2
    qb = _head_rows(r.q[group // n_groups], N_FOX_HEADS, FOX_WIDTH).astype(BF16)
    x = r.lfbuf[slot].reshape(n * N_FOX_HEADS, PAGE_SIZE)
    wt = _dot3(x, r.tw[...])
    later = r.carry[...]
    scores = [None] * n
    for j in reversed(range(n)):
        rows = slice(j * N_FOX_HEADS, (j + 1) * N_FOX_HEADS)
        bias = wt[rows, :PAGE_SIZE] + later
        later = later + wt[rows, PAGE_SIZE:]
        scores[j] = jnp.dot(qb, r.kbuf[slot, j].astype(BF16), preferred_element_type=F32) + bias
    r.carry[...] = later
    z = jnp.concatenate(scores, axis=-1)
    m_old = r.m[...]
    m_new = jnp.maximum(m_old, jnp.max(z, axis=-1, keepdims=True))
    alpha = jnp.exp(m_old - m_new)
    p = jnp.exp(z - m_new[:, 0:1])
    r.l[...] = alpha * r.l[...] + jnp.sum(p, axis=-1, keepdims=True)
    r.cn[...] = alpha * r.cn[...]
    r.m[...] = m_new
    for h in range(N_FOX_HEADS):
        a = r.acc[h] * alpha[h:h + 1, :]
        for j in range(n):
            a = a + r.vbuf[slot, j, h] * p[h:h + 1, j * PAGE_SIZE:(j + 1) * PAGE_SIZE]
        r.acc[h] = a


def _decode_finish(r, group):
    n_groups = r.pt.shape[1] // PAGES_PER_STEP
    seq_i = group // n_groups

    @pl.when(group % n_groups == n_groups - 1)
    def _():
        acc2d = r.acc[...].reshape(FOX_WIDTH, PAGE_SIZE)
        ones = jnp.ones((BF16_SUBLANES, PAGE_SIZE), BF16)
        past = _dot3_nt(ones, acc2d)[0:1]
        new = _heads_to_row(r.cn[...], FOX_WIDTH) * r.vnew[seq_i]
        r.orow[...] = (past + new) / _heads_to_row(r.l[...], FOX_WIDTH)
        cp = pltpu.make_async_copy(r.orow, r.o_hbm.at[seq_i], r.osem)
        cp.start()
        cp.wait()


def _decode_operands(page_table, q, knew, vnew, lfnew, lf_cache, k_cache, v_cache, o_prev):
    del page_table
    n = PAGES_PER_STEP
    pos = jnp.arange(PAGE_SIZE)
    tw = jnp.concatenate([_ones_where(pos[:, None] > pos[None, :]),
                          jnp.ones((PAGE_SIZE, PAGE_SIZE), BF16)], axis=1)
    hbm = pl.BlockSpec(memory_space=pl.ANY)
    small = (q, knew, vnew, lfnew, tw)
    args = small + (lf_cache, k_cache, v_cache, o_prev)
    in_specs = [_resident(a.shape) for a in small] + [hbm] * 4
    scratch = ([pltpu.VMEM((2, n, N_FOX_HEADS, PAGE_SIZE), F32),
                pltpu.VMEM((2, n, FOX_WIDTH, PAGE_SIZE), F32),
                pltpu.VMEM((2, n, N_FOX_HEADS, HEAD_DIM, PAGE_SIZE), F32),
                pltpu.SemaphoreType.DMA((2, 3, n)),
                pltpu.VMEM((1, FOX_WIDTH), F32),
                pltpu.SemaphoreType.DMA(())]
               + [pltpu.VMEM((N_FOX_HEADS, LANES), F32)] * 4
               + [pltpu.VMEM((N_FOX_HEADS, HEAD_DIM, PAGE_SIZE), F32)])
    assert len(args) == N_DECODE_INPUTS and len(scratch) == N_DECODE_SCRATCH
    return args, in_specs, hbm, jax.ShapeDtypeStruct(o_prev.shape, o_prev.dtype), scratch


def _mem_decode_kernel(q_ref, mkt_ref, mvt_ref, o_ref):
    for b in range(q_ref.shape[0]):
        qb = _head_rows(q_ref[b], SUBLANES, MEM_WIDTH).astype(BF16)
        s = jnp.dot(qb, mkt_ref[b].astype(BF16), preferred_element_type=F32)
        p = jnp.exp(s - jnp.max(s, axis=-1, keepdims=True))
        l = jnp.sum(p, axis=-1, keepdims=True)
        o = lax.dot_general(p.astype(BF16), mvt_ref[b].astype(BF16), _NT, preferred_element_type=F32) / l
        o = jnp.where(_head_mask(SUBLANES, MEM_WIDTH), o, 0.0)
        o_ref[b] = jnp.sum(o, axis=0, keepdims=True).astype(o_ref.dtype)


def _mem_decode(qm, mkt, mvt):
    bsz = qm.shape[0]
    per_step = MEM_DECODE_SEQS
    return pl.pallas_call(
        _mem_decode_kernel,
        grid=(bsz // per_step,),
        in_specs=[pl.BlockSpec((per_step, 1, MEM_WIDTH), lambda b: (b, 0, 0)),
                  pl.BlockSpec((per_step, MEM_WIDTH, N_MEM), lambda b: (b, 0, 0)),
                  pl.BlockSpec((per_step, MEM_WIDTH, N_MEM), lambda b: (b, 0, 0))],
        out_specs=pl.BlockSpec((per_step, 1, MEM_WIDTH), lambda b: (b, 0, 0)),
        out_shape=jax.ShapeDtypeStruct((bsz, 1, MEM_WIDTH), BF16),
        compiler_params=_params(1),
        name="mem_decode",
    )(qm, mkt, mvt)


def _conv_norm_act(conv, gcn_ref, bcn_ref):
    mu = jnp.mean(conv, axis=-1, keepdims=True)
    d = conv - mu
    var = jnp.mean(d * d, axis=-1, keepdims=True)
    y = d * lax.rsqrt(var + EPS) * gcn_ref[...] + bcn_ref[...]
    return y * _sigmoid(y)


def _merge_project(x, yfox, yb, yc, g_ref, wa_ref, wb_ref, wc_ref, wo_ref, gffn_ref, x1_ref, h2_ref):
    ga = g_ref[:, :D_MODEL].astype(F32)
    gb = g_ref[:, D_MODEL:2 * D_MODEL].astype(F32)
    gc = g_ref[:, 2 * D_MODEL:].astype(F32)
    merged = (ga * jnp.dot(yfox, wa_ref[...], preferred_element_type=F32)
              + gb * jnp.dot(yb.astype(BF16), wb_ref[...], preferred_element_type=F32)
              + gc * jnp.dot(yc, wc_ref[...], preferred_element_type=F32))
    x1 = x + jnp.dot(merged.astype(BF16), wo_ref[...], preferred_element_type=F32)
    x1_ref[...] = x1
    h2_ref[...] = (_rms_scale(x1) * gffn_ref[...]).astype(BF16)


def _tail_prompt_kernel(x_ref, yfox_ref, u_ref, uhalo_ref, qm_ref, g_ref, mkt_ref, mv_ref,
                        wdw_ref, gcn_ref, bcn_ref, wa_ref, wb_ref, wc_ref, wo_ref, gffn_ref,
                        x1_ref, h2_ref, ubuf_ref, ushift_ref, *, tiles_per_seq):
    i = pl.program_id(0)
    tm = x_ref.shape[0]
    halo = uhalo_ref[...]
    ubuf_ref[0:CONV_HALO, :] = jnp.where(i % tiles_per_seq == 0, jnp.zeros_like(halo), halo)
    ubuf_ref[CONV_HALO:, :] = u_ref[...]
    base = CONV_HALO - (CONV_WIDTH - 1)
    reach = CONV_HALO - SUBLANES + tm
    for phase in range(1, SUBLANES):
        ushift_ref[phase, 0:reach, :] = ubuf_ref[phase:phase + reach, :]
    chunk = 64
    convs = []
    for r0 in range(0, tm, chunk):
        acc = jnp.zeros((chunk, CONV_CH), F32)
        for j in range(CONV_WIDTH):
            phase, lo = (base + j) % SUBLANES, (base + j) // SUBLANES * SUBLANES + r0
            rows = ubuf_ref[lo:lo + chunk, :] if phase == 0 else ushift_ref[phase, lo:lo + chunk, :]
            acc = acc + wdw_ref[j:j + 1, :] * rows
        convs.append(acc)
    yb = _conv_norm_act(jnp.concatenate(convs, axis=0), gcn_ref, bcn_ref)

    qm = qm_ref[...]
    lane = lax.broadcasted_iota(jnp.int32, (tm, MEM_WIDTH), 1)
    mkt = mkt_ref[0]
    mv = mv_ref[...]
    yc = jnp.zeros((tm, MEM_WIDTH), F32)
    for h in range(N_MEM_HEADS):
        in_head = (lane >= h * HEAD_DIM) & (lane < (h + 1) * HEAD_DIM)
        qh = jnp.where(in_head, qm, jnp.zeros((), qm.dtype))
        s = jnp.dot(qh, mkt, preferred_element_type=F32)
        p = jnp.exp(s - jnp.max(s, axis=-1, keepdims=True))
        o = jnp.dot(p.astype(BF16), mv, preferred_element_type=F32) / jnp.sum(p, axis=-1, keepdims=True)
        yc = jnp.where(in_head, o, yc)

    _merge_project(x_ref[...], yfox_ref[...], yb, yc.astype(BF16), g_ref,
                   wa_ref, wb_ref, wc_ref, wo_ref, gffn_ref, x1_ref, h2_ref)


def _tail_sample_kernel(x_ref, yfox_ref, u_ref, state_ref, yc_ref, g_ref,
                        wdw_ref, gcn_ref, bcn_ref, wa_ref, wb_ref, wc_ref, wo_ref, gffn_ref,
                        x1_ref, h2_ref):
    conv = wdw_ref[CONV_WIDTH - 1:CONV_WIDTH, :] * u_ref[...]
    for j in range(CONV_WIDTH - 1):
        conv = conv + wdw_ref[j:j + 1, :] * state_ref[j]
    yb = _conv_norm_act(conv, gcn_ref, bcn_ref)
    _merge_project(x_ref[...], yfox_ref[...], yb, yc_ref[...], g_ref,
                   wa_ref, wb_ref, wc_ref, wo_ref, gffn_ref, x1_ref, h2_ref)


def _tail_prompt(x2d, yfox, u, qm, g, mktb, mvb, tiles_per_seq, tm, *ws):
    n = x2d.shape[0]
    rows = lambda w: pl.BlockSpec((tm, w), lambda i: (i, 0))
    halo_blocks = tm // CONV_HALO
    return pl.pallas_call(
        functools.partial(_tail_prompt_kernel, tiles_per_seq=tiles_per_seq),
        grid=(n // tm,),
        in_specs=[rows(D_MODEL), rows(FOX_WIDTH), rows(CONV_CH),
                  pl.BlockSpec((CONV_HALO, CONV_CH), lambda i: (jnp.maximum(i * halo_blocks - 1, 0), 0)),
                  rows(MEM_WIDTH), rows(3 * D_MODEL),
                  pl.BlockSpec((1, MEM_WIDTH, N_MEM), lambda i: (i // tiles_per_seq, 0, 0)),
                  pl.BlockSpec((N_MEM, MEM_WIDTH), lambda i: (i // tiles_per_seq, 0))]
                 + [_resident(w.shape) for w in ws],
        out_specs=[rows(D_MODEL), rows(D_MODEL)],
        out_shape=[jax.ShapeDtypeStruct((n, D_MODEL), F32), jax.ShapeDtypeStruct((n, D_MODEL), BF16)],
        scratch_shapes=[pltpu.VMEM((CONV_HALO + tm, CONV_CH), F32),
                        pltpu.VMEM((SUBLANES, CONV_HALO + tm, CONV_CH), F32)],
        compiler_params=_params(1),
        name="tail_prompt",
    )(x2d, yfox, u, u, qm, g, mktb, mvb, *ws)


def _tail_sample(x2d, yfox, u, state_t, yc, g, *ws):
    n = x2d.shape[0]
    full = lambda a: pl.BlockSpec(a.shape, lambda i: (0,) * a.ndim)
    ins = (x2d, yfox, u, state_t, yc, g)
    return pl.pallas_call(
        _tail_sample_kernel,
        grid=(1,),
        in_specs=[full(a) for a in ins] + [_resident(w.shape) for w in ws],
        out_specs=[pl.BlockSpec((n, D_MODEL), lambda i: (0, 0))] * 2,
        out_shape=[jax.ShapeDtypeStruct((n, D_MODEL), F32), jax.ShapeDtypeStruct((n, D_MODEL), BF16)],
        compiler_params=_params(1),
        name="tail_sample",
    )(*ins, *ws)


def _ffn_finish(x1_ref, acc_ref, gfin_ref, y_ref):
    x2 = x1_ref[...] + acc_ref[...]
    y_ref[...] = _rms_scale(x2) * gfin_ref[...]


def _ffn_prompt_kernel(pt_ref, h2_ref, halo_ref, x1_ref, wg_ref, wu_ref, wd_ref, wcv_ref, gfin_ref, *refs,
                       tiles_per_seq, first, last):
    y_ref, gtail_ref, o_hbm, hcat_ref, gbuf_ref, acc_ref = refs[N_DECODE_INPUTS:N_DECODE_INPUTS + 6]
    dec = _decode_refs(pt_ref, refs[:N_DECODE_INPUTS], o_hbm, refs[N_DECODE_INPUTS + 6:])
    n_chunks = wg_ref.shape[0]
    hosted_per_tile = n_chunks // FFN_CHUNKS_PER_GROUP
    i = pl.program_id(0)
    tm = h2_ref.shape[0]
    halo = halo_ref[...]
    hcat_ref[0:FFN_HALO, :] = jnp.where(i % tiles_per_seq == 0, jnp.zeros_like(halo), halo)
    hcat_ref[FFN_HALO:, :] = h2_ref[...]
    acc_ref[...] = jnp.zeros_like(acc_ref)

    def chunk(c, group):
        if group is not None:
            _decode_fetch(dec, group, first, last)
            _decode_compute(dec, group, first)
        gbuf_ref[...] = jnp.dot(hcat_ref[...], wg_ref[c], preferred_element_type=F32)
        up = jnp.dot(h2_ref[...], wu_ref[c], preferred_element_type=F32)
        w = wcv_ref[c]
        cv = (w[0:1, :] * gbuf_ref[FFN_HALO - 2:FFN_HALO - 2 + tm, :]
              + w[1:2, :] * gbuf_ref[FFN_HALO - 1:FFN_HALO - 1 + tm, :]
              + w[2:3, :] * gbuf_ref[FFN_HALO:FFN_HALO + tm, :])
        act = (cv * _sigmoid(cv) * up).astype(BF16)
        acc_ref[...] += jnp.dot(act, wd_ref[c], preferred_element_type=F32)
        gtail_ref[0, c] = gbuf_ref[FFN_HALO + tm - SUBLANES:FFN_HALO + tm, :]
        if group is not None:
            _decode_finish(dec, group)

    def plain_chunk(c, carry):
        chunk(c, None)
        return carry

    for k in range(hosted_per_tile):
        c0 = k * FFN_CHUNKS_PER_GROUP
        chunk(c0, first + i * hosted_per_tile + k)
        lax.fori_loop(c0 + 1, c0 + FFN_CHUNKS_PER_GROUP, plain_chunk, 0)
    _ffn_finish(x1_ref, acc_ref, gfin_ref, y_ref)


def _ffn_sample_kernel(h2_ref, state_ref, x1_ref, wg_ref, wu_ref, wd_ref, wcv_ref, gfin_ref,
                       y_ref, gt_ref, acc_ref):
    acc_ref[...] = jnp.zeros_like(acc_ref)

    def chunk(c, carry):
        h2 = h2_ref[...]
        gt = jnp.dot(h2, wg_ref[c], preferred_element_type=F32)
        up = jnp.dot(h2, wu_ref[c], preferred_element_type=F32)
        w = wcv_ref[c]
        cv = w[0:1, :] * state_ref[c, 0] + w[1:2, :] * state_ref[c, 1] + w[2:3, :] * gt
        act = (cv * _sigmoid(cv) * up).astype(BF16)
        acc_ref[...] += jnp.dot(act, wd_ref[c], preferred_element_type=F32)
        gt_ref[c] = gt
        return carry

    lax.fori_loop(0, wg_ref.shape[0], chunk, 0)
    _ffn_finish(x1_ref, acc_ref, gfin_ref, y_ref)


def _ffn_prompt_steps(n_rows, tm):
    assert (D_FF // FF_CHUNK) % FFN_CHUNKS_PER_GROUP == 0
    return (n_rows // tm) * (D_FF // FF_CHUNK // FFN_CHUNKS_PER_GROUP)


def _ffn_prompt(h2, x1, tiles_per_seq, tm, wg, wu, wd, wcv, gfin, page_table, decode_in, first):
    n = h2.shape[0]
    nt = n // tm
    n_chunks = wg.shape[0]
    last = first + _ffn_prompt_steps(n, tm) - 1
    d_args, d_specs, d_out_spec, d_out_shape, d_scratch = _decode_operands(page_table, *decode_in)
    rows = lambda w: pl.BlockSpec((tm, w), lambda i, pt: (i, 0))
    halo_blocks = tm // FFN_HALO
    grid_spec = pltpu.PrefetchScalarGridSpec(
        num_scalar_prefetch=1,
        grid=(nt,),
        in_specs=[rows(D_MODEL),
                  pl.BlockSpec((FFN_HALO, D_MODEL), lambda i, pt: (jnp.maximum(i * halo_blocks - 1, 0), 0)),
                  rows(D_MODEL), _resident(wg.shape), _resident(wu.shape), _resident(wd.shape),
                  _resident(wcv.shape), _resident(gfin.shape)] + d_specs,
        out_specs=[rows(D_MODEL), pl.BlockSpec((1, n_chunks, SUBLANES, FF_CHUNK), lambda i, pt: (i, 0, 0, 0)),
                   d_out_spec],
        scratch_shapes=[pltpu.VMEM((FFN_HALO + tm, D_MODEL), BF16),
                        pltpu.VMEM((FFN_HALO + tm, FF_CHUNK), F32),
                        pltpu.VMEM((tm, D_MODEL), F32)] + d_scratch,
    )
    return pl.pallas_call(
        functools.partial(_ffn_prompt_kernel, tiles_per_seq=tiles_per_seq, first=first, last=last),
        grid_spec=grid_spec,
        out_shape=[jax.ShapeDtypeStruct((n, D_MODEL), F32),
                   jax.ShapeDtypeStruct((nt, n_chunks, SUBLANES, FF_CHUNK), F32), d_out_shape],
        input_output_aliases={8 + len(d_args): 2},
        compiler_params=_params(1),
        name="ffn_prompt",
    )(page_table, h2, h2, x1, wg, wu, wd, wcv, gfin, *d_args)


def _ffn_sample(h2, state_c, x1, wg, wu, wd, wcv, gfin):
    n = h2.shape[0]
    n_chunks = wg.shape[0]
    full = lambda a: pl.BlockSpec(a.shape, lambda i: (0,) * a.ndim)
    return pl.pallas_call(
        _ffn_sample_kernel,
        grid=(1,),
        in_specs=[full(h2), full(state_c), full(x1), _resident(wg.shape), _resident(wu.shape),
                  _resident(wd.shape), _resident(wcv.shape), _resident(gfin.shape)],
        out_specs=[pl.BlockSpec((n, D_MODEL), lambda i: (0, 0)),
                   pl.BlockSpec((n_chunks, n, FF_CHUNK), lambda i: (0, 0, 0))],
        out_shape=[jax.ShapeDtypeStruct((n, D_MODEL), F32),
                   jax.ShapeDtypeStruct((n_chunks, n, FF_CHUNK), F32)],
        scratch_shapes=[pltpu.VMEM((n, D_MODEL), F32)],
        compiler_params=_params(1),
        name="ffn_sample",
    )(h2, state_c, x1, wg, wu, wd, wcv, gfin)


def _pad_rows(a, rows):
    return jnp.concatenate([a, jnp.zeros((rows - a.shape[0],) + a.shape[1:], a.dtype)], axis=0)


def _heads_last(xt, batch, n_heads, n_pos):
    return xt.reshape(batch, n_heads, HEAD_DIM, n_pos).transpose(0, 3, 1, 2)


def kernel(x_prompt, x_sample, cache_fox_k, cache_fox_v, cache_fox_logf, state_conv, state_ffn_conv, cache_mem_k, cache_mem_v, page_table, mem_prompt, g_mix, w_in, b_f, w_conv_dw, g_conv_norm, b_conv_norm, g_mem, w_mem_kv, w_a, w_b, w_c, w_o, g_ffn, w_ffn_in, w_ffn_conv, w_down, g_final):
    depth = g_mix.shape[0]
    assert depth == 1, "kernel is written for the single-layer trunk the problem states"
    bp, seq, _ = x_prompt.shape
    bs, dec_seq, _ = x_sample.shape
    assert dec_seq == 1 and seq % FFN_ROW_TILE == 0 and seq % ATTN_BLOCK == 0 and bs % LANES == 0
    n_pool = cache_fox_k.shape[1]
    n_pages = page_table.shape[1]
    assert n_pages % PAGES_PER_STEP == 0 and cache_fox_k.shape[2] == PAGE_SIZE
    l = 0

    w = w_in[l]
    wt = w.T
    o_f = 3 * FOX_WIDTH
    o_glu = o_f + N_FOX_HEADS
    o_qm = o_glu + 2 * CONV_CH
    o_gate = o_qm + MEM_WIDTH
    front_w = (
        g_mix[l][None, :],
        w[:, :FOX_WIDTH].astype(BF16),
        wt[FOX_WIDTH:o_f].astype(BF16),
        _pad_rows(wt[o_f:o_glu], BF16_SUBLANES).astype(BF16),
        b_f[l][:, None],
        w[:, o_glu:o_qm].astype(BF16),
        w[:, o_qm:o_gate].astype(BF16),
        w[:, o_gate:].astype(BF16),
    )
    tail_w = (_pad_rows(w_conv_dw[l], CONV_HALO), g_conv_norm[l][None, :], b_conv_norm[l][None, :],
              w_a[l].astype(BF16), w_b[l].astype(BF16), w_c[l].astype(BF16), w_o[l].astype(BF16),
              g_ffn[l][None, :])
    n_chunks = D_FF // FF_CHUNK
    wfi = w_ffn_in[l]
    wg = wfi[:, :D_FF].reshape(D_MODEL, n_chunks, FF_CHUNK).transpose(1, 0, 2).astype(BF16)
    wu = wfi[:, D_FF:].reshape(D_MODEL, n_chunks, FF_CHUNK).transpose(1, 0, 2).astype(BF16)
    wd = w_down[l].reshape(n_chunks, FF_CHUNK, D_MODEL).astype(BF16)
    wcv = _pad_rows(w_ffn_conv[l], SUBLANES).reshape(SUBLANES, n_chunks, FF_CHUNK).transpose(1, 0, 2)
    ffn_w = (wg, wu, wd, wcv, g_final[None, :])

    xs2 = x_sample.reshape(bs, D_MODEL)
    (q_s, kt_s, _, vt_s, _, lft_s, _, u_s, qm_s, g_s, krow_s, vrow_s) = _front(xs2, 1, bs, True, *front_w)
    row3 = lambda a: a.reshape(bs, 1, a.shape[-1])
    decode_in = [row3(q_s), row3(krow_s), row3(vrow_s), lft_s[0].T[:, :, None],
                 cache_fox_logf[l].transpose(0, 2, 1),
                 cache_fox_k[l].transpose(0, 2, 3, 1).reshape(n_pool, FOX_WIDTH, PAGE_SIZE),
                 cache_fox_v[l].transpose(0, 2, 3, 1),
                 jnp.zeros((bs, 1, FOX_WIDTH), F32)]
    groups_per_seq = n_pages // PAGES_PER_STEP
    in_fox = _fox_prompt_steps(bp, seq)
    in_ffn = _ffn_prompt_steps(bp * seq, FFN_ROW_TILE)
    assert in_fox % groups_per_seq == 0 and in_fox + in_ffn == bs * groups_per_seq, \
        "the prompt kernels' steps must cover the decode page groups exactly"

    xp2 = x_prompt.reshape(bp * seq, D_MODEL)
    q_p, kt_p, ktb_p, vt_p, vb_p, lft_p, ct_p, u_p, qm_p, g_p = _front(xp2, bp, FRONT_ROW_TILE, False, *front_w)
    wmt = w_mem_kv[l].T.astype(BF16)
    mkt_p, mvt_p, mktb_p, mvb_p = _memkv(mem_prompt.reshape(bp * N_MEM, D_MODEL), bp, g_mem[l][None, :],
                                         wmt, w_mem_kv[l][:, MEM_WIDTH:].astype(BF16))
    yfox_p, decode_in[-1] = _fox_prompt(q_p, ktb_p, vb_p, ct_p.reshape(bp, FOX_WIDTH // LANES, 2, seq), bp, seq,
                                        page_table, decode_in, 0)
    x1_p, h2_p = _tail_prompt(xp2, yfox_p, u_p, qm_p, g_p, mktb_p, mvb_p, seq // ROW_TILE, ROW_TILE, *tail_w)
    ffn_tiles = seq // FFN_ROW_TILE
    y_p, gtail_p, yfox_s = _ffn_prompt(h2_p, x1_p, ffn_tiles, FFN_ROW_TILE, *ffn_w,
                                       page_table, decode_in, in_fox)

    yc_s = _mem_decode(row3(qm_s),
                       cache_mem_k[l].transpose(0, 2, 3, 1).reshape(bs, MEM_WIDTH, N_MEM),
                       cache_mem_v[l].transpose(0, 2, 3, 1).reshape(bs, MEM_WIDTH, N_MEM))
    x1_s, h2_s = _tail_sample(xs2, yfox_s.reshape(bs, FOX_WIDTH).astype(BF16), u_s,
                              state_conv[l].transpose(1, 0, 2), yc_s.reshape(bs, MEM_WIDTH), g_s, *tail_w)
    ffn_state_c = state_ffn_conv[l].reshape(bs, FFN_CONV_WIDTH - 1, n_chunks, FF_CHUNK).transpose(2, 1, 0, 3)
    y_s, gt_s = _ffn_sample(h2_s, ffn_state_c, x1_s, *ffn_w)

    y_prompt = y_p.reshape(bp, seq, D_MODEL)
    y_sample = y_s.reshape(bs, 1, D_MODEL)
    fox_k_p = _heads_last(kt_p, bp, N_FOX_HEADS, seq)[None]
    fox_v_p = _heads_last(vt_p, bp, N_FOX_HEADS, seq)[None]
    fox_logf_p = lft_p.transpose(0, 2, 1)[None]
    conv_state_p = u_p.reshape(bp, seq, CONV_CH)[:, seq - (CONV_WIDTH - 1):][None]
    gt_last = gtail_p.reshape(bp, ffn_tiles, n_chunks, SUBLANES, FF_CHUNK)[
        :, -1, :, SUBLANES - (FFN_CONV_WIDTH - 1):, :]
    ffn_state_p = gt_last.transpose(0, 2, 1, 3).reshape(1, bp, FFN_CONV_WIDTH - 1, D_FF)
    mem_k_p = _heads_last(mkt_p, bp, N_MEM_HEADS, N_MEM)[None]
    mem_v_p = _heads_last(mvt_p, bp, N_MEM_HEADS, N_MEM)[None]
    fox_k_s = _heads_last(kt_s, 1, N_FOX_HEADS, bs).reshape(1, bs, 1, N_FOX_HEADS, HEAD_DIM)
    fox_v_s = _heads_last(vt_s, 1, N_FOX_HEADS, bs).reshape(1, bs, 1, N_FOX_HEADS, HEAD_DIM)
    fox_logf_s = lft_s[0].T.reshape(1, bs, 1, N_FOX_HEADS)
    conv_state_s = jnp.concatenate([state_conv[l][:, 1:], u_s[:, None, :]], axis=1)[None]
    gt_new = gt_s.transpose(1, 0, 2).reshape(bs, 1, D_FF)
    ffn_state_s = jnp.concatenate([state_ffn_conv[l][:, 1:], gt_new], axis=1)[None]
    return (y_prompt, y_sample, fox_k_p, fox_v_p, fox_logf_p, conv_state_p, ffn_state_p, mem_k_p, mem_v_p,
            fox_k_s, fox_v_s, fox_logf_s, conv_state_s, ffn_state_s)
```
